```python
import jax, jax.numpy as jnp
from jax import lax
import numpy as np

D_MODEL = 2048
BATCH = 4
SEQ = 8192
DEPTH = 1

GRID_W = 64
CTX_LEN = 256
N_Q_HEADS = 16
N_KV_HEADS = 4
HEAD_DIM = 128
ATTN_Q_W = N_Q_HEADS * HEAD_DIM
ATTN_KV_W = N_KV_HEADS * HEAD_DIM
ROPE_THETA = 10000.0
Q_BLOCK = 128
RWKV_HEADS = 16
RWKV_HEAD = 64
RWKV_W = RWKV_HEADS * RWKV_HEAD
W_LORA = 64
A_LORA = 64
G_LORA = 128
RWKV_SPLITS = (RWKV_W, RWKV_W, RWKV_W, W_LORA, W_LORA, A_LORA, A_LORA, G_LORA)
RWKV_IN_W = 3 * RWKV_W + 2 * W_LORA + 2 * A_LORA + G_LORA
IN_SPLITS = (ATTN_Q_W, ATTN_KV_W, ATTN_KV_W, RWKV_IN_W, D_MODEL, D_MODEL)
IN_W = ATTN_Q_W + 2 * ATTN_KV_W + RWKV_IN_W + 2 * D_MODEL
D_FF = 5632
N_MOD = 9
EPS = 1e-6
GN_EPS = 64e-5

kernel_name = "hybrid_gqa_rwkv7_macaron_dit"


def split_cols(t, sizes):
    return jnp.split(t, np.cumsum(sizes)[:-1].tolist(), axis=-1)


def rms_norm(x, g):
    xf = x.astype(jnp.float32)
    y = xf * lax.rsqrt(jnp.mean(xf * xf, axis=-1, keepdims=True) + EPS)
    return y.astype(x.dtype) * g


def modulate(h, shift, scale):
    return h * (1 + scale) + shift


def swiglu(h, w_in, w_out):
    gt, up = jnp.split(h @ w_in, 2, axis=-1)
    return (jax.nn.silu(gt) * up) @ w_out


def ffn_sublayer(x, shift, scale, gate, g, w_in, w_out):
    return x + 0.5 * gate * swiglu(modulate(rms_norm(x, g), shift, scale), w_in, w_out)


def head_rms(t, g):
    tf = t.astype(jnp.float32)
    return (tf * lax.rsqrt(jnp.mean(tf * tf, axis=-1, keepdims=True) + EPS)).astype(t.dtype) * g


def grid_rope_tables(n_tok, dtype):
    rows = n_tok // GRID_W
    row = jnp.repeat(jnp.arange(rows, dtype=jnp.int32), GRID_W)
    col = jnp.tile(jnp.arange(GRID_W, dtype=jnp.int32), rows)
    axis_dim = HEAD_DIM // 2
    inv = ROPE_THETA ** (-jnp.arange(0, axis_dim, 2, dtype=jnp.float32) / axis_dim)
    ang_r = row.astype(jnp.float32)[:, None] * inv[None]
    ang_c = col.astype(jnp.float32)[:, None] * inv[None]
    tab = lambda a: (jnp.cos(a)[None, :, None, :].astype(dtype), jnp.sin(a)[None, :, None, :].astype(dtype))
    return tab(ang_r), tab(ang_c)


def rope_1d(t, cos, sin):
    t1, t2 = jnp.split(t, 2, axis=-1)
    return jnp.concatenate([t1 * cos - t2 * sin, t2 * cos + t1 * sin], axis=-1)


def axial_rope(t, tables):
    (cr, sr), (cc, sc) = tables
    t_row, t_col = jnp.split(t, 2, axis=-1)
    return jnp.concatenate([rope_1d(t_row, cr, sr), rope_1d(t_col, cc, sc)], axis=-1)


def gqa_blocks(q, k, v):
    B, T, _, hd = q.shape
    G = N_Q_HEADS // N_KV_HEADS
    nb = T // Q_BLOCK
    qb = jnp.moveaxis(q.reshape(B, nb, Q_BLOCK, N_KV_HEADS, G, hd), 1, 0)
    scale = hd ** -0.5

    def one(qblk):
        s = jnp.einsum('bqhgd,bkhd->bhgqk', qblk, k, preferred_element_type=jnp.float32) * scale
        p = jax.nn.softmax(s, axis=-1)
        return jnp.einsum('bhgqk,bkhd->bqhgd', p.astype(v.dtype), v)

    o = lax.map(one, qb)
    return jnp.moveaxis(o, 0, 1).reshape(B, T, N_Q_HEADS * hd)


def centered_shift(p):
    prev = jnp.pad(p[:, :-1], ((0, 0), (1, 0), (0, 0)))
    nxt = jnp.pad(p[:, 1:], ((0, 0), (0, 1), (0, 0)))
    return 0.5 * (prev + nxt)


def rwkv_prepare(p, mu, w0, w2, a0, a2, g2, k_k, k_a):
    B, T, _ = p.shape
    p = p + mu * (centered_shift(p) - p)
    r, k, v, lw_f, lw_b, la_f, la_b, lg = split_cols(p, RWKV_SPLITS)
    lw = jnp.stack([lw_f, lw_b])
    la = jnp.stack([la_f, la_b])
    w_raw = (w0[:, None, None, :] + jnp.einsum('zbtr,zrc->zbtc', jnp.tanh(lw), w2)).astype(jnp.float32)
    decay = jnp.exp(-jnp.exp(-jax.nn.softplus(-w_raw) - 0.5))
    a = jax.nn.sigmoid((a0[:, None, None, :] + jnp.einsum('zbtr,zrc->zbtc', la, a2)).astype(jnp.float32))
    g = jax.nn.sigmoid(lg) @ g2
    hs = lambda t: t.reshape(t.shape[:-1] + (RWKV_HEADS, RWKV_HEAD))
    rf, kf, vf = hs(r.astype(jnp.float32)), hs(k.astype(jnp.float32)), hs(v.astype(jnp.float32))
    kk = kf * hs(k_k.astype(jnp.float32))
    kk = kk / jnp.maximum(jnp.sqrt(jnp.sum(kk * kk, axis=-1, keepdims=True)), 1e-12)
    a_h = hs(a)
    k_mod = kf[None] * (1 + (a_h - 1) * hs(k_a.astype(jnp.float32)))
    b = kk[None] * a_h
    return rf, hs(decay), k_mod, vf, kk, b, g


def rwkv_scan(S0, feats, emit):
    rf, decay, k_mod, vf, kk, b, _ = feats
    shared = lambda t: jnp.moveaxis(jnp.stack([t, t[:, ::-1]]), 2, 0)
    perdir = lambda t: jnp.moveaxis(jnp.stack([t[0], t[1][:, ::-1]]), 2, 0)
    xs = (shared(rf), perdir(decay), perdir(k_mod), shared(vf), shared(-kk), perdir(b))

    def step(S, inp):
        r_t, w_t, k_t, v_t, a_t, b_t = inp
        sa = jnp.einsum('zbhvk,zbhk->zbhv', S, a_t)
        S = S * w_t[..., None, :] + sa[..., None] * b_t[..., None, :] + v_t[..., None] * k_t[..., None, :]
        y = jnp.einsum('zbhvk,zbhk->zbhv', S, r_t) if emit else None
        return S, y

    S, ys = lax.scan(step, S0, xs)
    if not emit:
        return S, None
    ys = jnp.moveaxis(ys, 0, 2)
    return S, ys[0] + ys[1][:, ::-1]


def rwkv_output(y, feats, r_k, ln_w, ln_b, dtype):
    rf, _, k_mod, vf, _, _, g = feats
    B, T, H, N = y.shape
    mu = jnp.mean(y, axis=-1, keepdims=True)
    var = jnp.mean(jnp.square(y - mu), axis=-1, keepdims=True)
    yn = ((y - mu) * lax.rsqrt(var + GN_EPS)).reshape(B, T, H * N) * ln_w + ln_b
    bonus = jnp.sum(rf[None] * k_mod * r_k.astype(jnp.float32), axis=(0, -1))[..., None] * vf
    return ((yn + bonus.reshape(B, T, H * N)) * g).astype(dtype)


def merge(ga, gr, attn_o, rwkv_o, w_oa, w_or, w_out):
    return (jax.nn.sigmoid(ga) * (attn_o @ w_oa) + jax.nn.sigmoid(gr) * (rwkv_o @ w_or)) @ w_out


def setup_inputs(seed: int = 0) -> dict:
    key = jax.random.key(seed)
    ks = jax.random.split(key, 36)
    f32 = jnp.float32
    L, D = DEPTH, D_MODEL
    nrm = lambda k, shape, s: jax.random.normal(k, shape, f32) * s
    return {
        "x": nrm(ks[0], (BATCH, SEQ, D), 1.0),
        "c": nrm(ks[1], (BATCH, D), 1.0),
        "ctx": nrm(ks[2], (BATCH, CTX_LEN, D), 1.0),
        "c_ctx": nrm(ks[3], (D,), 1.0),
        "w_mod": nrm(ks[4], (L, D, N_MOD * D), D ** -0.5),
        "b_mod": nrm(ks[5], (L, N_MOD * D), 0.01),
        "norm_ffn1": 1.0 + nrm(ks[6], (L, D), 0.01),
        "ffn1_w_in": nrm(ks[7], (L, D, 2 * D_FF), D ** -0.5),
        "ffn1_w_out": nrm(ks[8], (L, D_FF, D), D_FF ** -0.5),
        "norm_mix": 1.0 + nrm(ks[9], (L, D), 0.01),
        "w_in": nrm(ks[10], (L, D, IN_W), D ** -0.5),
        "q_norm": 1.0 + nrm(ks[11], (L, HEAD_DIM), 0.01),
        "k_norm": 1.0 + nrm(ks[12], (L, HEAD_DIM), 0.01),
        "rwkv_mu": jax.random.uniform(ks[13], (L, RWKV_IN_W), f32),
        "w0": jax.random.uniform(ks[14], (L, 2, RWKV_W), f32, -6.0, -1.0),
        "w2": nrm(ks[15], (L, 2, W_LORA, RWKV_W), 0.5 * W_LORA ** -0.5),
        "a0": nrm(ks[16], (L, 2, RWKV_W), 0.1),
        "a2": nrm(ks[17], (L, 2, A_LORA, RWKV_W), A_LORA ** -0.5),
        "g2": nrm(ks[18], (L, G_LORA, RWKV_W), G_LORA ** -0.5),
        "k_k": 0.85 + nrm(ks[19], (L, RWKV_W), 0.02),
        "k_a": 1.0 + nrm(ks[20], (L, RWKV_W), 0.02),
        "r_k": nrm(ks[21], (L, RWKV_HEADS, RWKV_HEAD), 0.1),
        "ln_x_w": 1.0 + nrm(ks[22], (L, RWKV_W), 0.01),
        "ln_x_b": nrm(ks[23], (L, RWKV_W), 0.01),
        "w_oa": nrm(ks[24], (L, ATTN_Q_W, D), ATTN_Q_W ** -0.5),
        "w_or": nrm(ks[25], (L, RWKV_W, D), RWKV_W ** -0.5),
        "w_out": nrm(ks[26], (L, D, D), D ** -0.5),
        "norm_ffn2": 1.0 + nrm(ks[27], (L, D), 0.01),
        "ffn2_w_in": nrm(ks[28], (L, D, 2 * D_FF), D ** -0.5),
        "ffn2_w_out": nrm(ks[29], (L, D_FF, D), D_FF ** -0.5),
        "norm_final": 1.0 + nrm(ks[30], (D,), 0.01),
    }


def reference(x, c, ctx, c_ctx, w_mod, b_mod, norm_ffn1, ffn1_w_in, ffn1_w_out, norm_mix, w_in,
              q_norm, k_norm, rwkv_mu, w0, w2, a0, a2, g2, k_k, k_a, r_k, ln_x_w, ln_x_b,
              w_oa, w_or, w_out, norm_ffn2, ffn2_w_in, ffn2_w_out, norm_final):
    B, S, D = x.shape
    tables = grid_rope_tables(S, x.dtype)
    cx = ctx
    hs = lambda t, h: t.reshape(t.shape[:-1] + (h, HEAD_DIM))
    for i in range(DEPTH):
        is_last = i == DEPTH - 1
        mod = (jax.nn.silu(c) @ w_mod[i] + b_mod[i]).reshape(B, N_MOD, 1, D)
        mod_c = (jax.nn.silu(c_ctx) @ w_mod[i] + b_mod[i]).reshape(N_MOD, D)

        x = ffn_sublayer(x, mod[:, 0], mod[:, 1], mod[:, 2], norm_ffn1[i], ffn1_w_in[i], ffn1_w_out[i])
        cx = ffn_sublayer(cx, mod_c[0], mod_c[1], mod_c[2], norm_ffn1[i], ffn1_w_in[i], ffn1_w_out[i])

        h = modulate(rms_norm(x, norm_mix[i]), mod[:, 3], mod[:, 4])
        hc = modulate(rms_norm(cx, norm_mix[i]), mod_c[3], mod_c[4])
        q, k, v, p_rw, ga, gr = split_cols(h @ w_in[i], IN_SPLITS)
        qc, kc, vc, p_rw_c, ga_c, gr_c = split_cols(hc @ w_in[i], IN_SPLITS)

        q = axial_rope(head_rms(hs(q, N_Q_HEADS), q_norm[i]), tables)
        k = axial_rope(head_rms(hs(k, N_KV_HEADS), k_norm[i]), tables)
        qc = head_rms(hs(qc, N_Q_HEADS), q_norm[i])
        kc = head_rms(hs(kc, N_KV_HEADS), k_norm[i])
        v, vc = hs(v, N_KV_HEADS), hs(vc, N_KV_HEADS)
        attn_lat = gqa_blocks(q, jnp.concatenate([k, kc], axis=1), jnp.concatenate([v, vc], axis=1))

        rw_args = (rwkv_mu[i], w0[i], w2[i], a0[i], a2[i], g2[i], k_k[i], k_a[i])
        feats_c = rwkv_prepare(p_rw_c, *rw_args)
        S0 = jnp.zeros((2, B, RWKV_HEADS, RWKV_HEAD, RWKV_HEAD), jnp.float32)
        S_ctx, y_c = rwkv_scan(S0, feats_c, emit=not is_last)
        feats = rwkv_prepare(p_rw, *rw_args)
        _, y = rwkv_scan(S_ctx, feats, emit=True)
        rwkv_lat = rwkv_output(y, feats, r_k[i], ln_x_w[i], ln_x_b[i], x.dtype)

        x = x + mod[:, 5] * merge(ga, gr, attn_lat, rwkv_lat, w_oa[i], w_or[i], w_out[i])

        if not is_last:
            attn_c = gqa_blocks(qc, kc, vc)
            rwkv_c = rwkv_output(y_c, feats_c, r_k[i], ln_x_w[i], ln_x_b[i], cx.dtype)
            cx = cx + mod_c[5] * merge(ga_c, gr_c, attn_c, rwkv_c, w_oa[i], w_or[i], w_out[i])
            cx = ffn_sublayer(cx, mod_c[6], mod_c[7], mod_c[8], norm_ffn2[i], ffn2_w_in[i], ffn2_w_out[i])

        x = ffn_sublayer(x, mod[:, 6], mod[:, 7], mod[:, 8], norm_ffn2[i], ffn2_w_in[i], ffn2_w_out[i])

    return rms_norm(x, norm_final)
```

```python
import functools
import math

import jax
import jax.numpy as jnp
from jax import lax
from jax.experimental import pallas as pl
from jax.experimental.pallas import tpu as pltpu

F32 = jnp.float32
BF16 = jnp.bfloat16

N_Q_HEADS = 16
N_KV_HEADS = 4
HEAD_DIM = 128
Q_PER_KV = N_Q_HEADS // N_KV_HEADS
RWKV_HEADS = 16
RWKV_HEAD = 64
RWKV_W = RWKV_HEADS * RWKV_HEAD
W_LORA = 64
A_LORA = 64
G_LORA = 128
RWKV_IN_W = 3 * RWKV_W + 2 * W_LORA + 2 * A_LORA + G_LORA
GRID_W = 64
ROPE_THETA = 10000.0
N_MOD = 9
EPS = 1e-6
GN_EPS = 64e-5
DECAY_RATE = math.exp(-0.5)

LANES = 128
SUBLANES = 8
VMEM_BYTES_V7X = 64 * 1024 * 1024
VMEM_LIMIT = 60000 * 1024

CHUNK = 64
PAIR = 2 * RWKV_HEAD
assert PAIR == LANES


def _tile(n, pref, mult):
    if n <= pref:
        return n
    best = None
    for t in range(mult, pref + 1, mult):
        if n % t == 0:
            best = t
    assert best is not None, (n, pref, mult)
    return best


def _params(sem, vmem=None):
    return pltpu.CompilerParams(dimension_semantics=sem, vmem_limit_bytes=min(vmem or VMEM_LIMIT, VMEM_LIMIT))


def _split_bf16(x, n):
    terms, rem = [], x
    for _ in range(n):
        t = rem.astype(BF16)
        terms.append(t)
        rem = rem - t.astype(F32)
    return terms


def _dotg(a, b, dims):
    return lax.dot_general(a, b, (dims, ((), ())), preferred_element_type=F32)


def _mm(a, b, dims=((1,), (0,)), na=1, nb=1):
    at = _split_bf16(a, na) if a.dtype != BF16 else [a]
    bt = _split_bf16(b, nb) if b.dtype != BF16 else [b]
    order = max(len(at), len(bt))
    acc = None
    for i, x in enumerate(at):
        for j, y in enumerate(bt):
            if i + j < order:
                d = _dotg(x, y, dims)
                acc = d if acc is None else acc + d
    return acc


def _rmsnorm(x, g):
    ms = jnp.mean(x * x, axis=-1, keepdims=True)
    return x * lax.rsqrt(ms + EPS) * g


def _mod_kernel(c_ref, w_ref, b_ref, o_ref):
    c = c_ref[...]
    s = (c * jax.nn.sigmoid(c)).astype(BF16)
    o_ref[...] = jnp.dot(s, w_ref[...].astype(BF16), preferred_element_type=F32) + b_ref[...]


def _mod_table(c, c_ctx, w_mod, b_mod):
    B, D = c.shape
    NO = w_mod.shape[1]
    rows = -(-(B + 1) // SUBLANES) * SUBLANES
    cin = jnp.zeros((rows, D), F32).at[:B].set(c).at[B].set(c_ctx)
    TN = _tile(NO, 1024, LANES)
    out = pl.pallas_call(
        _mod_kernel,
        grid=(NO // TN,),
        in_specs=[pl.BlockSpec((rows, D), lambda j: (0, 0)),
                  pl.BlockSpec((D, TN), lambda j: (0, j)),
                  pl.BlockSpec((1, TN), lambda j: (0, j))],
        out_specs=pl.BlockSpec((rows, TN), lambda j: (0, j)),
        out_shape=jax.ShapeDtypeStruct((rows, NO), F32),
        compiler_params=_params(("parallel",)),
        name="mod_table",
    )(cin, w_mod, b_mod.reshape(1, NO))
    return out[:B + 1].reshape(B + 1, N_MOD, D)


def _ffn_kernel(x_ref, mod_ref, g_ref, wg_ref, wu_ref, wo_ref, *rest, mod_base, final_norm):
    if final_norm:
        gf_ref, o_ref, h_scr, acc_scr = rest
    else:
        o_ref, h_scr, acc_scr = rest
    f = pl.program_id(1)

    @pl.when(f == 0)
    def _():
        shift = mod_ref[0, mod_base:mod_base + 1, :]
        scale = mod_ref[0, mod_base + 1:mod_base + 2, :]
        h = _rmsnorm(x_ref[...], g_ref[...]) * (1.0 + scale) + shift
        h_scr[...] = h.astype(BF16)
        acc_scr[...] = jnp.zeros_like(acc_scr)

    h = h_scr[...]
    gt = jnp.dot(h, wg_ref[...], preferred_element_type=F32)
    up = jnp.dot(h, wu_ref[...], preferred_element_type=F32)
    act = (gt * jax.nn.sigmoid(gt) * up).astype(BF16)
    acc_scr[...] += jnp.dot(act, wo_ref[...], preferred_element_type=F32)

    @pl.when(f == pl.num_programs(1) - 1)
    def _():
        gate = mod_ref[0, mod_base + 2:mod_base + 3, :]
        out = x_ref[...] + 0.5 * gate * acc_scr[...]
        if final_norm:
            out = _rmsnorm(out, gf_ref[...])
        o_ref[...] = out


def _ffn(x2, modtab, mod_row, mod_base, g, w_in, w_out, g_final=None):
    N, D = x2.shape
    F = w_out.shape[0]
    seq, mod_row = mod_row
    TM = _tile(seq, 512, SUBLANES)
    TF = _tile(F, 512, LANES)
    nf = F // TF
    in_specs = [pl.BlockSpec((TM, D), lambda i, f: (i, 0)),
                pl.BlockSpec((1, N_MOD, D), lambda i, f: (mod_row(i, TM), 0, 0)),
                pl.BlockSpec((1, D), lambda i, f: (0, 0)),
                pl.BlockSpec((D, TF), lambda i, f: (0, f)),
                pl.BlockSpec((D, TF), lambda i, f: (0, nf + f)),
                pl.BlockSpec((TF, D), lambda i, f: (f, 0))]
    args = [x2, modtab, g.reshape(1, D), w_in, w_in, w_out]
    if g_final is not None:
        in_specs.append(pl.BlockSpec((1, D), lambda i, f: (0, 0)))
        args.append(g_final.reshape(1, D))
    vmem = 2 * (2 * TM * D * 4 + 3 * D * TF * 2) + TM * D * 6 + 4 * TM * TF * 4
    return pl.pallas_call(
        functools.partial(_ffn_kernel, mod_base=mod_base, final_norm=g_final is not None),
        grid=(N // TM, nf),
        in_specs=in_specs,
        out_specs=pl.BlockSpec((TM, D), lambda i, f: (i, 0)),
        out_shape=jax.ShapeDtypeStruct((N, D), F32),
        scratch_shapes=[pltpu.VMEM((TM, D), BF16), pltpu.VMEM((TM, D), F32)],
        compiler_params=_params(("parallel", "arbitrary"), vmem + (8 << 20)),
        name="ffn",
    )(*args)


def _swap32(t):
    lane = lax.broadcasted_iota(jnp.int32, t.shape, t.ndim - 1)
    return jnp.where((lane % 64) >= 32, pltpu.roll(t, 32, t.ndim - 1), pltpu.roll(t, LANES - 32, t.ndim - 1))


def _proj_kernel(x_ref, mod_ref, g_ref, w_ref, *rest, mod_base, kind, out_scale):
    o_ref, h_scr = rest[-2:]
    extras = rest[:-2]
    j = pl.program_id(1)

    @pl.when(j == 0)
    def _():
        shift = mod_ref[0, mod_base:mod_base + 1, :]
        scale = mod_ref[0, mod_base + 1:mod_base + 2, :]
        h = _rmsnorm(x_ref[...], g_ref[...]) * (1.0 + scale) + shift
        h_scr[...] = h.astype(BF16)

    acc = jnp.dot(h_scr[...], w_ref[...], preferred_element_type=F32)
    if kind == "plain":
        o_ref[...] = acc.astype(o_ref.dtype)
    elif kind == "sigmoid":
        o_ref[...] = jax.nn.sigmoid(acc).astype(o_ref.dtype)
    else:
        hn = extras[0][...]
        for hd in range(acc.shape[1] // HEAD_DIM):
            t = acc[:, hd * HEAD_DIM:(hd + 1) * HEAD_DIM]
            t = _rmsnorm(t, hn)
            if kind == "headnorm_rope":
                t = t * extras[1][...] + _swap32(t) * extras[2][...]
            o_ref[:, hd * HEAD_DIM:(hd + 1) * HEAD_DIM] = (t * out_scale).astype(o_ref.dtype)


def _proj(x2, modtab, mod_row, mod_base, g, w, kind, out_dtype, extras=(), extra_specs=(), out_scale=1.0):
    N, D = x2.shape
    NO = w.shape[1]
    seq, mod_row = mod_row
    TM = _tile(seq, 1024, SUBLANES)
    TN = _tile(NO, 512, LANES)
    in_specs = [pl.BlockSpec((TM, D), lambda i, j: (i, 0)),
                pl.BlockSpec((1, N_MOD, D), lambda i, j: (mod_row(i, TM), 0, 0)),
                pl.BlockSpec((1, D), lambda i, j: (0, 0)),
                pl.BlockSpec((D, TN), lambda i, j: (0, j))]
    in_specs += [spec(TM) for spec in extra_specs]
    vmem = 2 * (TM * D * 4 + D * TN * 2 + TM * TN * 4 + 2 * TM * LANES * 4) + TM * D * 2 + 3 * TM * TN * 4 + 2 * TM * D * 4
    return pl.pallas_call(
        functools.partial(_proj_kernel, mod_base=mod_base, kind=kind, out_scale=out_scale),
        grid=(N // TM, NO // TN),
        in_specs=in_specs,
        out_specs=pl.BlockSpec((TM, TN), lambda i, j: (i, j)),
        out_shape=jax.ShapeDtypeStruct((N, NO), out_dtype),
        scratch_shapes=[pltpu.VMEM((TM, D), BF16)],
        compiler_params=_params(("parallel", "arbitrary"), vmem),
        name="proj_" + kind,
    )(x2, modtab, g.reshape(1, D), w, *extras)


def _rope_tables(n_tok):
    rows = n_tok // GRID_W
    row = jnp.repeat(jnp.arange(rows, dtype=jnp.int32), GRID_W).astype(F32)
    col = jnp.tile(jnp.arange(GRID_W, dtype=jnp.int32), rows).astype(F32)
    axis_dim = HEAD_DIM // 2
    inv = ROPE_THETA ** (-jnp.arange(0, axis_dim, 2, dtype=F32) / axis_dim)
    ang_r = row[:, None] * inv[None]
    ang_c = col[:, None] * inv[None]
    cr, sr, cc, sc = jnp.cos(ang_r), jnp.sin(ang_r), jnp.cos(ang_c), jnp.sin(ang_c)
    return (jnp.concatenate([cr, cr, cc, cc], axis=-1), jnp.concatenate([-sr, sr, -sc, sc], axis=-1))


def _attn_kernel(q_ref, k_ref, v_ref, o_ref, q_scr, m_scr, l_scr, acc_scr, *, TQ):
    ki = pl.program_id(3)

    @pl.when(ki == 0)
    def _():
        for h in range(Q_PER_KV):
            q_scr[h * TQ:(h + 1) * TQ, :] = q_ref[0, :, h * HEAD_DIM:(h + 1) * HEAD_DIM]
        m_scr[...] = jnp.full_like(m_scr, -jnp.inf)
        l_scr[...] = jnp.zeros_like(l_scr)
        acc_scr[...] = jnp.zeros_like(acc_scr)

    s = _dotg(q_scr[...], k_ref[0], ((1,), (1,)))
    m_prev = m_scr[...]
    m_new = jnp.maximum(m_prev, jnp.max(s, axis=-1, keepdims=True))
    alpha = jnp.exp(m_prev - m_new)
    p = jnp.exp(s - m_new)
    l_scr[...] = alpha * l_scr[...] + jnp.sum(p, axis=-1, keepdims=True)
    acc_scr[...] = alpha * acc_scr[...] + jnp.dot(p.astype(BF16), v_ref[0], preferred_element_type=F32)
    m_scr[...] = m_new

    @pl.when(ki == pl.num_programs(3) - 1)
    def _():
        o = acc_scr[...] / l_scr[...]
        for h in range(Q_PER_KV):
            o_ref[0, :, h * HEAD_DIM:(h + 1) * HEAD_DIM] = o[h * TQ:(h + 1) * TQ, :].astype(o_ref.dtype)


def _attention(q, k, v):
    B, T, _ = q.shape
    Tk = k.shape[1]
    TQ = _tile(T, 256, 16)
    TK = _tile(Tk, 768, LANES)
    GW = Q_PER_KV * HEAD_DIM
    R = Q_PER_KV * TQ
    vmem = 2 * (2 * TQ * GW * 2 + 2 * TK * HEAD_DIM * 2) + R * HEAD_DIM * (2 + 4 * 3) + 4 * R * TK * 4
    return pl.pallas_call(
        functools.partial(_attn_kernel, TQ=TQ),
        grid=(B, N_KV_HEADS, T // TQ, Tk // TK),
        in_specs=[pl.BlockSpec((1, TQ, GW), lambda b, g, qi, ki: (b, qi, g)),
                  pl.BlockSpec((1, TK, HEAD_DIM), lambda b, g, qi, ki: (b, ki, g)),
                  pl.BlockSpec((1, TK, HEAD_DIM), lambda b, g, qi, ki: (b, ki, g))],
        out_specs=pl.BlockSpec((1, TQ, GW), lambda b, g, qi, ki: (b, qi, g)),
        out_shape=jax.ShapeDtypeStruct(q.shape, BF16),
        scratch_shapes=[pltpu.VMEM((R, HEAD_DIM), BF16), pltpu.VMEM((R, 1), F32),
                        pltpu.VMEM((R, 1), F32), pltpu.VMEM((R, HEAD_DIM), F32)],
        compiler_params=_params(("parallel", "parallel", "parallel", "arbitrary"), vmem),
        name="gqa_flash",
    )(q, k, v)


def _segsum64(x, ones_bd):
    cols = []
    for c in range(x.shape[1] // LANES):
        cols.append(_mm(x[:, c * LANES:(c + 1) * LANES], ones_bd, na=3, nb=1))
    return jnp.concatenate(cols, axis=1)


def _prep_kernel(p_ref, pprev_ref, pnext_ref, mu_ref, w2_ref, w0_ref, a2_ref, a0_ref, g2_ref, kk_ref, ka_ref, rk_ref,
                 ones_ref, r_o, v_o, nkk_o, lw_o, km_o, bb_o, g_o, bv_o):
    p = p_ref[0]
    TT = p.shape[0]
    row = lax.broadcasted_iota(jnp.int32, (TT, 1), 0)
    prev = jnp.where(row == 0, pprev_ref[0, 0], pltpu.roll(p, 1, 0))
    nxt = jnp.where(row == TT - 1, pnext_ref[0, 0], pltpu.roll(p, TT - 1, 0))
    ps = p + mu_ref[...] * (0.5 * (prev + nxt) - p)
    W = RWKV_W
    r, k, v = ps[:, 0:W], ps[:, W:2 * W], ps[:, 2 * W:3 * W]
    lw = ps[:, 3 * W:3 * W + 2 * W_LORA]
    la = ps[:, 3 * W + 2 * W_LORA:3 * W + 2 * W_LORA + 2 * A_LORA]
    lg = ps[:, 3 * W + 2 * W_LORA + 2 * A_LORA:]
    ones_bd = ones_ref[...]
    w_raw = _mm(jnp.tanh(lw), w2_ref[...], na=2, nb=2) + w0_ref[...]
    logw = -DECAY_RATE * jax.nn.sigmoid(w_raw)
    a = jax.nn.sigmoid(_mm(la, a2_ref[...], na=2, nb=2) + a0_ref[...])
    g = _mm(jax.nn.sigmoid(lg), g2_ref[...], na=2, nb=2)
    kkv = k * kk_ref[...]
    kk = kkv / jnp.maximum(jnp.sqrt(_segsum64(kkv * kkv, ones_bd)), 1e-12)
    ka = ka_ref[...]
    kmod_sum = None
    for z in range(2):
        az = a[:, z * W:(z + 1) * W]
        kmod = k * (1.0 + (az - 1.0) * ka)
        lw_o[z, 0] = logw[:, z * W:(z + 1) * W]
        km_o[z, 0] = kmod
        bb_o[z, 0] = kk * az
        kmod_sum = kmod if kmod_sum is None else kmod_sum + kmod
    bonus = _segsum64(r * kmod_sum * rk_ref[...], ones_bd)
    r_o[0] = r
    v_o[0] = v
    nkk_o[0] = -kk
    g_o[0] = g
    bv_o[0] = bonus * v


def _blockdiag2(m):
    _, R, C = m.shape
    z = jnp.zeros((R, C), m.dtype)
    return jnp.concatenate([jnp.concatenate([m[0], z], axis=1), jnp.concatenate([z, m[1]], axis=1)], axis=0)


def _rwkv_prep(p, mu, w0, w2, a0, a2, g2, k_k, k_a, r_k):
    B, T, WI = p.shape
    TT = _tile(T, 256, SUBLANES)
    nT = T // TT
    zero = jnp.zeros((B, 1, WI), F32)
    pprev = jnp.concatenate([zero, p[:, TT - 1:T - 1:TT]], axis=1).reshape(B, nT, 1, WI)
    pnext = jnp.concatenate([p[:, TT::TT], zero], axis=1).reshape(B, nT, 1, WI)
    W = RWKV_W
    ones_bd = _blockdiag2(jnp.ones((2, RWKV_HEAD, RWKV_HEAD), F32))
    vec = lambda t, n: pl.BlockSpec((1, n), lambda b, i: (0, 0))
    full = lambda a: pl.BlockSpec(a.shape, lambda b, i: (0,) * a.ndim)
    w2bd, a2bd = _blockdiag2(w2), _blockdiag2(a2)
    ins = [p, pprev, pnext, mu.reshape(1, WI), w2bd, w0.reshape(1, 2 * W), a2bd, a0.reshape(1, 2 * W), g2,
           k_k.reshape(1, W), k_a.reshape(1, W), r_k.reshape(1, W), ones_bd]
    in_specs = [pl.BlockSpec((1, TT, WI), lambda b, i: (b, i, 0)),
                pl.BlockSpec((1, 1, 1, WI), lambda b, i: (b, i, 0, 0)),
                pl.BlockSpec((1, 1, 1, WI), lambda b, i: (b, i, 0, 0))] + [full(a) for a in ins[3:]]
    tok = jax.ShapeDtypeStruct((B, T, W), F32)
    tokz = jax.ShapeDtypeStruct((2, B, T, W), F32)
    s1 = pl.BlockSpec((1, TT, W), lambda b, i: (b, i, 0))
    s2 = pl.BlockSpec((2, 1, TT, W), lambda b, i: (0, b, i, 0))
    return pl.pallas_call(
        _prep_kernel,
        grid=(B, nT),
        in_specs=in_specs,
        out_specs=[s1, s1, s1, s2, s2, s2, s1, s1],
        out_shape=[tok, tok, tok, tokz, tokz, tokz, tok, tok],
        compiler_params=_params(("parallel", "parallel")),
        name="rwkv_prep",
    )(*ins)


def _sm(x, lo):
    return jnp.concatenate([jnp.where(lo, x, 0.0), jnp.where(lo, 0.0, x)], axis=0)


def _scan_kernel(r_ref, v_ref, a_ref, lw_ref, km_ref, bb_ref, s0_ref, y_ref, sN_ref, s_scr, *, HPG, NPI):
    z = pl.program_id(0)
    c = pl.program_id(3)
    C = CHUNK

    @pl.when(c == 0)
    def _():
        s_scr[...] = s0_ref[0, 0]

    sgn = 1 - 2 * z
    ri = lax.broadcasted_iota(jnp.int32, (2 * C, 2 * C), 0)
    ci = lax.broadcasted_iota(jnp.int32, (2 * C, 2 * C), 1)
    order = (ri - ci) * sgn
    strict = order > 0
    incl = order >= 0
    tr = lax.broadcasted_iota(jnp.int32, (C, C), 0)
    tc = lax.broadcasted_iota(jnp.int32, (C, C), 1)
    tri = jnp.where((tr - tc) * sgn >= 0, 1.0, 0.0).astype(BF16)
    eye = jnp.where(ri == ci, 1.0, 0.0)
    lo = lax.broadcasted_iota(jnp.int32, (C, LANES), 1) < RWKV_HEAD

    for hp in range(HPG):
        sl = slice(hp * LANES, (hp + 1) * LANES)
        lw = lw_ref[0, 0, :, sl]
        cum = _mm(tri, lw, na=1, nb=3)
        tot = jnp.sum(lw, axis=0, keepdims=True)
        dec_in = jnp.exp(cum - lw)
        dec_out = jnp.exp(cum)
        inv = jnp.exp(-cum)
        rest = jnp.exp(tot - cum)
        a_t = _sm(a_ref[0, :, sl] * dec_in, lo)
        r_t = _sm(r_ref[0, :, sl] * dec_out, lo)
        bb = bb_ref[0, 0, :, sl]
        km = km_ref[0, 0, :, sl]
        b_t = _sm(bb * inv, lo)
        k_t = _sm(km * inv, lo)
        bk_end = jnp.concatenate([_sm(bb * rest, lo), _sm(km * rest, lo)], axis=0)
        v_s = _sm(v_ref[0, :, sl], lo)

        ar = jnp.concatenate([a_t, r_t], axis=0)
        G = _mm(ar, jnp.concatenate([b_t, k_t], axis=0), dims=((1,), (1,)), na=NPI[0], nb=NPI[0])
        L = jnp.where(strict, G[:2 * C, :2 * C], 0.0)
        Lak = jnp.where(strict, G[:2 * C, 2 * C:], 0.0)
        Grb = jnp.where(incl, G[2 * C:, :2 * C], 0.0)
        Grk = jnp.where(incl, G[2 * C:, 2 * C:], 0.0)

        T = eye + jnp.where((ri >> 1) == (ci >> 1), L, 0.0)
        for lvl in range(1, int(math.log2(C))):
            off = ((ri >> (lvl + 1)) == (ci >> (lvl + 1))) & ((ri >> lvl) != (ci >> lvl))
            T = T + _mm(_mm(T, jnp.where(off, L, 0.0), na=NPI[1], nb=NPI[1]), T, na=NPI[1], nb=NPI[1])

        S = s_scr[hp]
        ms = _mm(ar, S, dims=((1,), (1,)), na=NPI[2], nb=NPI[2])
        X = ms[:2 * C] + _mm(Lak, v_s, na=NPI[3], nb=NPI[3])
        U = _mm(T, X, na=NPI[4], nb=NPI[4])
        uv = jnp.concatenate([U, v_s], axis=0)
        Y = ms[2 * C:] + _mm(jnp.concatenate([Grb, Grk], axis=1), uv, na=NPI[5], nb=NPI[5])
        y_ref[0, 0, :, sl] = Y[:C] + Y[C:]
        s_scr[hp] = S * jnp.exp(tot) + _mm(uv, bk_end, dims=((0,), (0,)), na=NPI[6], nb=NPI[6])

    @pl.when(c == pl.num_programs(3) - 1)
    def _():
        sN_ref[0, 0] = s_scr[...]


SCAN_TERMS = (1, 1, 1, 1, 2, 1, 1)


def _rwkv_scan(r, v, nkk, lw, km, bb, s0, npi=SCAN_TERMS):
    B, T, W = r.shape
    C = CHUNK
    NC = T // C
    HPG = 4
    NG = W // (HPG * LANES)
    GW = HPG * LANES
    tb = lambda z, c: c + z * (NC - 1 - 2 * c)
    shared = pl.BlockSpec((1, C, GW), lambda z, b, g, c: (b, tb(z, c), g))
    perdir = pl.BlockSpec((1, 1, C, GW), lambda z, b, g, c: (z, b, tb(z, c), g))
    state = pl.BlockSpec((1, 1, HPG, LANES, LANES), lambda z, b, g, c: (z, b, g, 0, 0))
    return pl.pallas_call(
        functools.partial(_scan_kernel, HPG=HPG, NPI=npi),
        grid=(2, B, NG, NC),
        in_specs=[shared, shared, shared, perdir, perdir, perdir, state],
        out_specs=[perdir, state],
        out_shape=[jax.ShapeDtypeStruct((2, B, T, W), F32), jax.ShapeDtypeStruct(s0.shape, F32)],
        scratch_shapes=[pltpu.VMEM((HPG, LANES, LANES), F32)],
        compiler_params=_params(("parallel", "parallel", "parallel", "arbitrary")),
        name="rwkv_scan",
    )(r, v, nkk, lw, km, bb, s0)


def _rwkv_out_kernel(y_ref, bv_ref, g_ref, lnw_ref, lnb_ref, ones_ref, o_ref):
    y = y_ref[0, 0] + y_ref[1, 0]
    ones_bd = ones_ref[...]
    inv_n = 1.0 / RWKV_HEAD
    mu = _segsum64(y, ones_bd) * inv_n
    d = y - mu
    var = _segsum64(d * d, ones_bd) * inv_n
    yn = d * lax.rsqrt(var + GN_EPS) * lnw_ref[...] + lnb_ref[...]
    o_ref[0] = ((yn + bv_ref[0]) * g_ref[0]).astype(o_ref.dtype)


def _rwkv_out(y, bv, g, ln_w, ln_b):
    _, B, T, W = y.shape
    TT = _tile(T, 512, SUBLANES)
    ones_bd = _blockdiag2(jnp.ones((2, RWKV_HEAD, RWKV_HEAD), F32))
    s1 = pl.BlockSpec((1, TT, W), lambda b, i: (b, i, 0))
    return pl.pallas_call(
        _rwkv_out_kernel,
        grid=(B, T // TT),
        in_specs=[pl.BlockSpec((2, 1, TT, W), lambda b, i: (0, b, i, 0)), s1, s1,
                  pl.BlockSpec((1, W), lambda b, i: (0, 0)), pl.BlockSpec((1, W), lambda b, i: (0, 0)),
                  pl.BlockSpec((LANES, LANES), lambda b, i: (0, 0))],
        out_specs=s1,
        out_shape=jax.ShapeDtypeStruct((B, T, W), BF16),
        compiler_params=_params(("parallel", "parallel")),
        name="rwkv_out",
    )(y, bv, g, ln_w.reshape(1, W), ln_b.reshape(1, W), ones_bd)


def _merge_kernel(ao_ref, ro_ref, woa_ref, wor_ref, ga_ref, gr_ref, o_ref):
    ta = jnp.dot(ao_ref[...], woa_ref[...], preferred_element_type=F32)
    tr = jnp.dot(ro_ref[...], wor_ref[...], preferred_element_type=F32)
    o_ref[...] = (ga_ref[...].astype(F32) * ta + gr_ref[...].astype(F32) * tr).astype(o_ref.dtype)


def _merge(attn_o, rwkv_o, w_oa, w_or, gates):
    N, DA = attn_o.shape
    DR = rwkv_o.shape[1]
    D = w_oa.shape[1]
    TM = _tile(N, 1024, 16)
    TN = _tile(D, 512, LANES)
    nj = D // TN
    vmem = 2 * (TM * (DA + DR) * 2 + (DA + DR) * TN * 2 + 3 * TM * TN * 2) + 4 * TM * TN * 4
    return pl.pallas_call(
        _merge_kernel,
        grid=(N // TM, nj),
        in_specs=[pl.BlockSpec((TM, DA), lambda i, j: (i, 0)),
                  pl.BlockSpec((TM, DR), lambda i, j: (i, 0)),
                  pl.BlockSpec((DA, TN), lambda i, j: (0, j)),
                  pl.BlockSpec((DR, TN), lambda i, j: (0, j)),
                  pl.BlockSpec((TM, TN), lambda i, j: (i, j)),
                  pl.BlockSpec((TM, TN), lambda i, j: (i, nj + j))],
        out_specs=pl.BlockSpec((TM, TN), lambda i, j: (i, j)),
        out_shape=jax.ShapeDtypeStruct((N, D), BF16),
        compiler_params=_params(("parallel", "parallel"), vmem),
        name="merge",
    )(attn_o, rwkv_o, w_oa, w_or, gates, gates)


def _outproj_kernel(t_ref, w_ref, x_ref, mod_ref, o_ref, *, mod_base):
    acc = jnp.dot(t_ref[...], w_ref[...], preferred_element_type=F32)
    o_ref[...] = x_ref[...] + mod_ref[0, mod_base:mod_base + 1, :] * acc


def _outproj(t, w, x2, modtab, mod_row, mod_base):
    N, K = t.shape
    D = w.shape[1]
    seq, mod_row = mod_row
    TM = _tile(seq, 1024, 16)
    TN = _tile(D, 512, LANES)
    vmem = 2 * (TM * K * 2 + K * TN * 2 + 2 * TM * TN * 4) + 2 * TM * TN * 4
    return pl.pallas_call(
        functools.partial(_outproj_kernel, mod_base=mod_base),
        grid=(N // TM, D // TN),
        in_specs=[pl.BlockSpec((TM, K), lambda i, j: (i, 0)),
                  pl.BlockSpec((K, TN), lambda i, j: (0, j)),
                  pl.BlockSpec((TM, TN), lambda i, j: (i, j)),
                  pl.BlockSpec((1, N_MOD, TN), lambda i, j: (mod_row(i, TM), 0, j))],
        out_specs=pl.BlockSpec((TM, TN), lambda i, j: (i, j)),
        out_shape=jax.ShapeDtypeStruct((N, D), F32),
        compiler_params=_params(("parallel", "parallel"), vmem),
        name="outproj",
    )(t, w, x2, modtab)


def kernel(x, c, ctx, c_ctx, w_mod, b_mod, norm_ffn1, ffn1_w_in, ffn1_w_out, norm_mix, w_in, q_norm, k_norm, rwkv_mu, w0, w2, a0, a2, g2, k_k, k_a, r_k, ln_x_w, ln_x_b, w_oa, w_or, w_out, norm_ffn2, ffn2_w_in, ffn2_w_out, norm_final):
    B, S, D = x.shape
    CT = ctx.shape[1]
    assert w_mod.shape[0] == 1, "single layer"
    ATT_Q, ATT_KV = N_Q_HEADS * HEAD_DIM, N_KV_HEADS * HEAD_DIM
    assert w_in.shape[2] == ATT_Q + 2 * ATT_KV + RWKV_IN_W + 2 * D
    assert S % CHUNK == 0 and CT % CHUNK == 0 and S % GRID_W == 0

    lat_row = (S, lambda i, tm: (i * tm) // S)
    ctx_row = (B * CT, lambda i, tm: B)
    bf = lambda w: w.astype(BF16)

    modtab = _mod_table(c, c_ctx, w_mod[0], b_mod[0])

    x2 = x.reshape(B * S, D)
    cx2 = ctx.reshape(B * CT, D)
    w1_in, w1_out = bf(ffn1_w_in[0]), bf(ffn1_w_out[0])
    x2 = _ffn(x2, modtab, lat_row, 0, norm_ffn1[0], w1_in, w1_out)
    cx2 = _ffn(cx2, modtab, ctx_row, 0, norm_ffn1[0], w1_in, w1_out)

    wi = w_in[0]
    o = 0
    wq = bf(wi[:, o:o + ATT_Q]); o += ATT_Q
    wk = bf(wi[:, o:o + ATT_KV]); o += ATT_KV
    wv = bf(wi[:, o:o + ATT_KV]); o += ATT_KV
    wr = bf(wi[:, o:o + RWKV_IN_W]); o += RWKV_IN_W
    wg = bf(wi[:, o:])
    cos_t, sin_t = _rope_tables(S)
    qn, kn = q_norm[0].reshape(1, HEAD_DIM), k_norm[0].reshape(1, HEAD_DIM)
    hn_spec = lambda tm: pl.BlockSpec((1, HEAD_DIM), lambda i, j: (0, 0))
    rope_spec = lambda tm: pl.BlockSpec((tm, HEAD_DIM), lambda i, j: (i % (S // tm), 0))
    nm = norm_mix[0]
    q = _proj(x2, modtab, lat_row, 3, nm, wq, "headnorm_rope", BF16, (qn, cos_t, sin_t),
              (hn_spec, rope_spec, rope_spec), out_scale=HEAD_DIM ** -0.5)
    k = _proj(x2, modtab, lat_row, 3, nm, wk, "headnorm_rope", BF16, (kn, cos_t, sin_t), (hn_spec, rope_spec, rope_spec))
    v = _proj(x2, modtab, lat_row, 3, nm, wv, "plain", BF16)
    p_rw = _proj(x2, modtab, lat_row, 3, nm, wr, "plain", F32)
    gates = _proj(x2, modtab, lat_row, 3, nm, wg, "sigmoid", BF16)
    kc = _proj(cx2, modtab, ctx_row, 3, nm, wk, "headnorm", BF16, (kn,), (hn_spec,))
    vc = _proj(cx2, modtab, ctx_row, 3, nm, wv, "plain", BF16)
    p_rw_c = _proj(cx2, modtab, ctx_row, 3, nm, wr, "plain", F32)

    k_all = jnp.concatenate([k.reshape(B, S, ATT_KV), kc.reshape(B, CT, ATT_KV)], axis=1)
    v_all = jnp.concatenate([v.reshape(B, S, ATT_KV), vc.reshape(B, CT, ATT_KV)], axis=1)
    attn_o = _attention(q.reshape(B, S, ATT_Q), k_all, v_all).reshape(B * S, ATT_Q)

    rw = (rwkv_mu[0], w0[0], w2[0], a0[0], a2[0], g2[0], k_k[0], k_a[0], r_k[0])
    rc, vcr, akc, lwc, kmc, bbc, _, _ = _rwkv_prep(p_rw_c.reshape(B, CT, RWKV_IN_W), *rw)
    s0 = jnp.zeros((2, B, RWKV_W // LANES, LANES, LANES), F32)
    _, s_ctx = _rwkv_scan(rc, vcr, akc, lwc, kmc, bbc, s0)
    rl, vl, akl, lwl, kml, bbl, gl, bvl = _rwkv_prep(p_rw.reshape(B, S, RWKV_IN_W), *rw)
    y, _ = _rwkv_scan(rl, vl, akl, lwl, kml, bbl, s_ctx)
    rwkv_o = _rwkv_out(y, bvl, gl, ln_x_w[0], ln_x_b[0]).reshape(B * S, RWKV_W)

    t = _merge(attn_o, rwkv_o, bf(w_oa[0]), bf(w_or[0]), gates)
    x2 = _outproj(t, bf(w_out[0]), x2, modtab, lat_row, 5)

    out = _ffn(x2, modtab, lat_row, 6, norm_ffn2[0], bf(ffn2_w_in[0]), bf(ffn2_w_out[0]), g_final=norm_final)
    return out.reshape(B, S, D)
```

```python
import functools
import math

import jax
import jax.numpy as jnp
from jax import lax
from jax.experimental import pallas as pl
from jax.experimental.pallas import tpu as pltpu

F32 = jnp.float32
BF16 = jnp.bfloat16

N_Q_HEADS = 16
N_KV_HEADS = 4
HEAD_DIM = 128
Q_PER_KV = N_Q_HEADS // N_KV_HEADS
RWKV_HEADS = 16
RWKV_HEAD = 64
RWKV_W = RWKV_HEADS * RWKV_HEAD
W_LORA = 64
A_LORA = 64
G_LORA = 128
RWKV_IN_W = 3 * RWKV_W + 2 * W_LORA + 2 * A_LORA + G_LORA
GRID_W = 64
ROPE_THETA = 10000.0
N_MOD = 9
EPS = 1e-6
GN_EPS = 64e-5
LOG2E = math.log2(math.e)
DECAY_RATE = math.exp(-0.5)

LANES = 128
SUBLANES = 8
VMEM_BYTES_V7X = 64 * 1024 * 1024
VMEM_LIMIT = 60000 * 1024

CHUNK = 64
PAIR = 2 * RWKV_HEAD
assert PAIR == LANES


def _tile(n, pref, mult):
    if n <= pref:
        return n
    best = None
    for t in range(mult, pref + 1, mult):
        if n % t == 0:
            best = t
    assert best is not None, (n, pref, mult)
    return best


def _params(sem, vmem=None):
    return pltpu.CompilerParams(dimension_semantics=sem, vmem_limit_bytes=min(vmem or VMEM_LIMIT, VMEM_LIMIT))


def _split_bf16(x, n):
    terms, rem = [], x
    for _ in range(n):
        t = rem.astype(BF16)
        terms.append(t)
        rem = rem - t.astype(F32)
    return terms


def _dotg(a, b, dims):
    return lax.dot_general(a, b, (dims, ((), ())), preferred_element_type=F32)


def _mm(a, b, dims=((1,), (0,)), na=1, nb=1):
    at = _split_bf16(a, na) if a.dtype != BF16 else [a]
    bt = _split_bf16(b, nb) if b.dtype != BF16 else [b]
    order = max(len(at), len(bt))
    acc = None
    for i, x in enumerate(at):
        for j, y in enumerate(bt):
            if i + j < order:
                d = _dotg(x, y, dims)
                acc = d if acc is None else acc + d
    return acc


def _rmsnorm(x, g):
    ms = jnp.mean(x * x, axis=-1, keepdims=True)
    return x * lax.rsqrt(ms + EPS) * g


def _mod_kernel(c_ref, w_ref, b_ref, o_ref):
    c = c_ref[...]
    s = (c * jax.nn.sigmoid(c)).astype(BF16)
    o_ref[...] = jnp.dot(s, w_ref[...].astype(BF16), preferred_element_type=F32) + b_ref[...]


def _mod_table(c, c_ctx, w_mod, b_mod):
    B, D = c.shape
    NO = w_mod.shape[1]
    rows = -(-(B + 1) // SUBLANES) * SUBLANES
    cin = jnp.zeros((rows, D), F32).at[:B].set(c).at[B].set(c_ctx)
    TN = _tile(NO, 1024, LANES)
    out = pl.pallas_call(
        _mod_kernel,
        grid=(NO // TN,),
        in_specs=[pl.BlockSpec((rows, D), lambda j: (0, 0)),
                  pl.BlockSpec((D, TN), lambda j: (0, j)),
                  pl.BlockSpec((1, TN), lambda j: (0, j))],
        out_specs=pl.BlockSpec((rows, TN), lambda j: (0, j)),
        out_shape=jax.ShapeDtypeStruct((rows, NO), F32),
        compiler_params=_params(("parallel",)),
        name="mod_table",
    )(cin, w_mod, b_mod.reshape(1, NO))
    return out[:B + 1].reshape(B + 1, N_MOD, D)


def _ffn_kernel(x_ref, mod_ref, g_ref, wg_ref, wu_ref, wo_ref, *rest, mod_base, final_norm):
    if final_norm:
        gf_ref, o_ref, h_scr, acc_scr = rest
    else:
        o_ref, h_scr, acc_scr = rest
    f = pl.program_id(1)

    @pl.when(f == 0)
    def _():
        shift = mod_ref[0, mod_base:mod_base + 1, :]
        scale = mod_ref[0, mod_base + 1:mod_base + 2, :]
        h = _rmsnorm(x_ref[...], g_ref[...]) * (1.0 + scale) + shift
        h_scr[...] = h.astype(BF16)
        acc_scr[...] = jnp.zeros_like(acc_scr)

    h = h_scr[...]
    gt = jnp.dot(h, wg_ref[...], preferred_element_type=F32)
    up = jnp.dot(h, wu_ref[...], preferred_element_type=F32)
    act = (gt * jax.nn.sigmoid(gt) * up).astype(BF16)
    acc_scr[...] += jnp.dot(act, wo_ref[...], preferred_element_type=F32)

    @pl.when(f == pl.num_programs(1) - 1)
    def _():
        gate = mod_ref[0, mod_base + 2:mod_base + 3, :]
        out = x_ref[...] + 0.5 * gate * acc_scr[...]
        if final_norm:
            out = _rmsnorm(out, gf_ref[...])
        o_ref[...] = out


def _ffn(x2, modtab, mod_row, mod_base, g, w_in, w_out, g_final=None):
    N, D = x2.shape
    F = w_out.shape[0]
    seq, mod_row = mod_row
    TM = _tile(seq, 512, SUBLANES)
    TF = _tile(F, 512, LANES)
    nf = F // TF
    in_specs = [pl.BlockSpec((TM, D), lambda i, f: (i, 0)),
                pl.BlockSpec((1, N_MOD, D), lambda i, f: (mod_row(i, TM), 0, 0)),
                pl.BlockSpec((1, D), lambda i, f: (0, 0)),
                pl.BlockSpec((D, TF), lambda i, f: (0, f)),
                pl.BlockSpec((D, TF), lambda i, f: (0, nf + f)),
                pl.BlockSpec((TF, D), lambda i, f: (f, 0))]
    args = [x2, modtab, g.reshape(1, D), w_in, w_in, w_out]
    if g_final is not None:
        in_specs.append(pl.BlockSpec((1, D), lambda i, f: (0, 0)))
        args.append(g_final.reshape(1, D))
    vmem = 2 * (2 * TM * D * 4 + 3 * D * TF * 2) + TM * D * 6 + 4 * TM * TF * 4
    return pl.pallas_call(
        functools.partial(_ffn_kernel, mod_base=mod_base, final_norm=g_final is not None),
        grid=(N // TM, nf),
        in_specs=in_specs,
        out_specs=pl.BlockSpec((TM, D), lambda i, f: (i, 0)),
        out_shape=jax.ShapeDtypeStruct((N, D), F32),
        scratch_shapes=[pltpu.VMEM((TM, D), BF16), pltpu.VMEM((TM, D), F32)],
        compiler_params=_params(("parallel", "arbitrary"), vmem + (8 << 20)),
        name="ffn",
    )(*args)


def _swap32(t):
    lane = lax.broadcasted_iota(jnp.int32, t.shape, t.ndim - 1)
    return jnp.where((lane % 64) >= 32, pltpu.roll(t, 32, t.ndim - 1), pltpu.roll(t, LANES - 32, t.ndim - 1))


def _proj_kernel(x_ref, mod_ref, g_ref, w_ref, *rest, mod_base, kind, out_scale):
    o_ref, h_scr = rest[-2:]
    extras = rest[:-2]
    j = pl.program_id(1)

    @pl.when(j == 0)
    def _():
        shift = mod_ref[0, mod_base:mod_base + 1, :]
        scale = mod_ref[0, mod_base + 1:mod_base + 2, :]
        h = _rmsnorm(x_ref[...], g_ref[...]) * (1.0 + scale) + shift
        h_scr[...] = h.astype(BF16)

    acc = jnp.dot(h_scr[...], w_ref[...], preferred_element_type=F32)
    if kind == "plain":
        o_ref[...] = acc.astype(o_ref.dtype)
    elif kind == "sigmoid":
        o_ref[...] = jax.nn.sigmoid(acc).astype(o_ref.dtype)
    else:
        hn = extras[0][...]
        for hd in range(acc.shape[1] // HEAD_DIM):
            t = acc[:, hd * HEAD_DIM:(hd + 1) * HEAD_DIM]
            t = _rmsnorm(t, hn)
            if kind == "headnorm_rope":
                t = t * extras[1][...] + _swap32(t) * extras[2][...]
            o_ref[:, hd * HEAD_DIM:(hd + 1) * HEAD_DIM] = (t * out_scale).astype(o_ref.dtype)


def _proj(x2, modtab, mod_row, mod_base, g, w, kind, out_dtype, extras=(), extra_specs=(), out_scale=1.0):
    N, D = x2.shape
    NO = w.shape[1]
    seq, mod_row = mod_row
    TM = _tile(seq, 1024, SUBLANES)
    TN = _tile(NO, 512, LANES)
    in_specs = [pl.BlockSpec((TM, D), lambda i, j: (i, 0)),
                pl.BlockSpec((1, N_MOD, D), lambda i, j: (mod_row(i, TM), 0, 0)),
                pl.BlockSpec((1, D), lambda i, j: (0, 0)),
                pl.BlockSpec((D, TN), lambda i, j: (0, j))]
    in_specs += [spec(TM) for spec in extra_specs]
    vmem = 2 * (TM * D * 4 + D * TN * 2 + TM * TN * 4 + 2 * TM * LANES * 4) + TM * D * 2 + 3 * TM * TN * 4 + 2 * TM * D * 4
    return pl.pallas_call(
        functools.partial(_proj_kernel, mod_base=mod_base, kind=kind, out_scale=out_scale),
        grid=(N // TM, NO // TN),
        in_specs=in_specs,
        out_specs=pl.BlockSpec((TM, TN), lambda i, j: (i, j)),
        out_shape=jax.ShapeDtypeStruct((N, NO), out_dtype),
        scratch_shapes=[pltpu.VMEM((TM, D), BF16)],
        compiler_params=_params(("parallel", "arbitrary"), vmem),
        name="proj_" + kind,
    )(x2, modtab, g.reshape(1, D), w, *extras)


def _rope_tables(n_tok):
    rows = n_tok // GRID_W
    row = jnp.repeat(jnp.arange(rows, dtype=jnp.int32), GRID_W).astype(F32)
    col = jnp.tile(jnp.arange(GRID_W, dtype=jnp.int32), rows).astype(F32)
    axis_dim = HEAD_DIM // 2
    inv = ROPE_THETA ** (-jnp.arange(0, axis_dim, 2, dtype=F32) / axis_dim)
    ang_r = row[:, None] * inv[None]
    ang_c = col[:, None] * inv[None]
    cr, sr, cc, sc = jnp.cos(ang_r), jnp.sin(ang_r), jnp.cos(ang_c), jnp.sin(ang_c)
    return (jnp.concatenate([cr, cr, cc, cc], axis=-1), jnp.concatenate([-sr, sr, -sc, sc], axis=-1))


def _attn_kernel(q_ref, k_ref, v_ref, o_ref, *, TQ, TK):
    n_k = k_ref.shape[1] // TK
    q = jnp.concatenate([q_ref[0, :, h * HEAD_DIM:(h + 1) * HEAD_DIM] for h in range(Q_PER_KV)], axis=0)
    R = q.shape[0]
    scores = lambda j: _dotg(q, k_ref[0, j * TK:(j + 1) * TK, :], ((1,), (1,)))
    m = jnp.full((R, 1), -jnp.inf, F32)
    l = jnp.zeros((R, 1), F32)
    acc = jnp.zeros((R, HEAD_DIM), F32)
    s_next = scores(0)
    for j in range(n_k):
        s = s_next
        if j + 1 < n_k:
            s_next = scores(j + 1)
        m_new = jnp.maximum(m, jnp.max(s, axis=-1, keepdims=True))
        alpha = jnp.exp2(m - m_new)
        p = jnp.exp2(s - m_new)
        l = alpha * l + jnp.sum(p, axis=-1, keepdims=True)
        acc = alpha * acc + jnp.dot(p.astype(BF16), v_ref[0, j * TK:(j + 1) * TK, :], preferred_element_type=F32)
        m = m_new
    o = acc / l
    for h in range(Q_PER_KV):
        o_ref[0, :, h * HEAD_DIM:(h + 1) * HEAD_DIM] = o[h * TQ:(h + 1) * TQ, :].astype(o_ref.dtype)


def _attention(q, k, v):
    B, T, _ = q.shape
    Tk = k.shape[1]
    TQ = _tile(T, 256, 16)
    TK = _tile(Tk, 768, LANES)
    GW = Q_PER_KV * HEAD_DIM
    R = Q_PER_KV * TQ
    vmem = 2 * (2 * TQ * GW * 2 + 2 * Tk * HEAD_DIM * 2) + R * HEAD_DIM * (2 + 4 * 3) + 10 * R * TK * 4
    return pl.pallas_call(
        functools.partial(_attn_kernel, TQ=TQ, TK=TK),
        grid=(B, N_KV_HEADS, T // TQ),
        in_specs=[pl.BlockSpec((1, TQ, GW), lambda b, g, qi: (b, qi, g)),
                  pl.BlockSpec((1, Tk, HEAD_DIM), lambda b, g, qi: (b, 0, g)),
                  pl.BlockSpec((1, Tk, HEAD_DIM), lambda b, g, qi: (b, 0, g))],
        out_specs=pl.BlockSpec((1, TQ, GW), lambda b, g, qi: (b, qi, g)),
        out_shape=jax.ShapeDtypeStruct(q.shape, BF16),
        compiler_params=_params(("parallel", "parallel", "arbitrary"), vmem),
        name="gqa_flash",
    )(q, k, v)


def _segsum64(x, ones_bd):
    cols = []
    for c in range(x.shape[1] // LANES):
        cols.append(_mm(x[:, c * LANES:(c + 1) * LANES], ones_bd, na=3, nb=1))
    return jnp.concatenate(cols, axis=1)


def _prep_kernel(p_ref, pprev_ref, pnext_ref, mu_ref, w2_ref, w0_ref, a2_ref, a0_ref, g2_ref, kk_ref, ka_ref, rk_ref,
                 ones_ref, r_o, v_o, nkk_o, lw_o, km_o, bb_o, g_o, bv_o):
    p = p_ref[0]
    TT = p.shape[0]
    row = lax.broadcasted_iota(jnp.int32, (TT, 1), 0)
    prev = jnp.where(row == 0, pprev_ref[0, 0], pltpu.roll(p, 1, 0))
    nxt = jnp.where(row == TT - 1, pnext_ref[0, 0], pltpu.roll(p, TT - 1, 0))
    ps = p + mu_ref[...] * (0.5 * (prev + nxt) - p)
    W = RWKV_W
    r, k, v = ps[:, 0:W], ps[:, W:2 * W], ps[:, 2 * W:3 * W]
    lw = ps[:, 3 * W:3 * W + 2 * W_LORA]
    la = ps[:, 3 * W + 2 * W_LORA:3 * W + 2 * W_LORA + 2 * A_LORA]
    lg = ps[:, 3 * W + 2 * W_LORA + 2 * A_LORA:]
    ones_bd = ones_ref[...]
    w_raw = _mm(jnp.tanh(lw), w2_ref[...], na=2, nb=2) + w0_ref[...]
    logw = -DECAY_RATE * jax.nn.sigmoid(w_raw)
    a = jax.nn.sigmoid(_mm(la, a2_ref[...], na=2, nb=2) + a0_ref[...])
    g = _mm(jax.nn.sigmoid(lg), g2_ref[...], na=2, nb=2)
    kkv = k * kk_ref[...]
    kk = kkv / jnp.maximum(jnp.sqrt(_segsum64(kkv * kkv, ones_bd)), 1e-12)
    ka = ka_ref[...]
    kmod_sum = None
    for z in range(2):
        az = a[:, z * W:(z + 1) * W]
        kmod = k * (1.0 + (az - 1.0) * ka)
        lw_o[z, 0] = logw[:, z * W:(z + 1) * W]
        km_o[z, 0] = kmod
        bb_o[z, 0] = kk * az
        kmod_sum = kmod if kmod_sum is None else kmod_sum + kmod
    bonus = _segsum64(r * kmod_sum * rk_ref[...], ones_bd)
    r_o[0] = r
    v_o[0] = v
    nkk_o[0] = -kk
    g_o[0] = g
    bv_o[0] = bonus * v


def _blockdiag2(m):
    _, R, C = m.shape
    z = jnp.zeros((R, C), m.dtype)
    return jnp.concatenate([jnp.concatenate([m[0], z], axis=1), jnp.concatenate([z, m[1]], axis=1)], axis=0)


def _rwkv_prep(p, mu, w0, w2, a0, a2, g2, k_k, k_a, r_k):
    B, T, WI = p.shape
    TT = _tile(T, 256, SUBLANES)
    nT = T // TT
    zero = jnp.zeros((B, 1, WI), F32)
    pprev = jnp.concatenate([zero, p[:, TT - 1:T - 1:TT]], axis=1).reshape(B, nT, 1, WI)
    pnext = jnp.concatenate([p[:, TT::TT], zero], axis=1).reshape(B, nT, 1, WI)
    W = RWKV_W
    ones_bd = _blockdiag2(jnp.ones((2, RWKV_HEAD, RWKV_HEAD), F32))
    vec = lambda t, n: pl.BlockSpec((1, n), lambda b, i: (0, 0))
    full = lambda a: pl.BlockSpec(a.shape, lambda b, i: (0,) * a.ndim)
    w2bd, a2bd = _blockdiag2(w2), _blockdiag2(a2)
    ins = [p, pprev, pnext, mu.reshape(1, WI), w2bd, w0.reshape(1, 2 * W), a2bd, a0.reshape(1, 2 * W), g2,
           k_k.reshape(1, W), k_a.reshape(1, W), r_k.reshape(1, W), ones_bd]
    in_specs = [pl.BlockSpec((1, TT, WI), lambda b, i: (b, i, 0)),
                pl.BlockSpec((1, 1, 1, WI), lambda b, i: (b, i, 0, 0)),
                pl.BlockSpec((1, 1, 1, WI), lambda b, i: (b, i, 0, 0))] + [full(a) for a in ins[3:]]
    tok = jax.ShapeDtypeStruct((B, T, W), F32)
    tokz = jax.ShapeDtypeStruct((2, B, T, W), F32)
    s1 = pl.BlockSpec((1, TT, W), lambda b, i: (b, i, 0))
    s2 = pl.BlockSpec((2, 1, TT, W), lambda b, i: (0, b, i, 0))
    return pl.pallas_call(
        _prep_kernel,
        grid=(B, nT),
        in_specs=in_specs,
        out_specs=[s1, s1, s1, s2, s2, s2, s1, s1],
        out_shape=[tok, tok, tok, tokz, tokz, tokz, tok, tok],
        compiler_params=_params(("parallel", "parallel")),
        name="rwkv_prep",
    )(*ins)


def _sm(x, lo):
    return jnp.concatenate([jnp.where(lo, x, 0.0), jnp.where(lo, 0.0, x)], axis=0)


def _scan_chain(z, refs, hp, s_scr, y_ref, terms):
    r_ref, v_ref, a_ref, lw_ref, km_ref, bb_ref = refs
    C = CHUNK
    sl = slice(hp * LANES, (hp + 1) * LANES)
    sgn = 1 - 2 * z
    ri = lax.broadcasted_iota(jnp.int32, (2 * C, 2 * C), 0)
    ci = lax.broadcasted_iota(jnp.int32, (2 * C, 2 * C), 1)
    strict = (ri - ci) * sgn > 0
    incl = (ri - ci) * sgn >= 0
    tr = lax.broadcasted_iota(jnp.int32, (C, C), 0)
    tc = lax.broadcasted_iota(jnp.int32, (C, C), 1)
    tri = jnp.where((tr - tc) * sgn >= 0, 1.0, 0.0).astype(BF16)
    eye = jnp.where(ri == ci, 1.0, 0.0)
    lo = lax.broadcasted_iota(jnp.int32, (C, LANES), 1) < RWKV_HEAD

    lw = lw_ref[0, 0, :, sl]
    cum = _mm(tri, lw, na=1, nb=3)
    yield
    tot = jnp.sum(lw, axis=0, keepdims=True)
    dec_in = jnp.exp(cum - lw)
    dec_out = jnp.exp(cum)
    inv = jnp.exp(-cum)
    rest = jnp.exp(tot - cum)
    a_t = _sm(a_ref[0, :, sl] * dec_in, lo)
    r_t = _sm(r_ref[0, :, sl] * dec_out, lo)
    bb = bb_ref[0, 0, :, sl]
    km = km_ref[0, 0, :, sl]
    b_t = _sm(bb * inv, lo)
    k_t = _sm(km * inv, lo)
    bk_end = jnp.concatenate([_sm(bb * rest, lo), _sm(km * rest, lo)], axis=0)
    v_s = _sm(v_ref[0, :, sl], lo)
    ar = jnp.concatenate([a_t, r_t], axis=0)
    G = _mm(ar, jnp.concatenate([b_t, k_t], axis=0), dims=((1,), (1,)), na=terms[0], nb=terms[0])
    S = s_scr[z, hp]
    ms = _mm(ar, S, dims=((1,), (1,)), na=terms[2], nb=terms[2])
    yield
    L = jnp.where(strict, G[:2 * C, :2 * C], 0.0)
    Lak = jnp.where(strict, G[:2 * C, 2 * C:], 0.0)
    Grb = jnp.where(incl, G[2 * C:, :2 * C], 0.0)
    Grk = jnp.where(incl, G[2 * C:, 2 * C:], 0.0)
    X = ms[:2 * C] + _mm(Lak, v_s, na=terms[3], nb=terms[3])

    T = eye + jnp.where((ri >> 1) == (ci >> 1), L, 0.0)
    for lvl in range(1, int(math.log2(C))):
        off = ((ri >> (lvl + 1)) == (ci >> (lvl + 1))) & ((ri >> lvl) != (ci >> lvl))
        TL = _mm(T, jnp.where(off, L, 0.0), na=terms[1], nb=terms[1])
        yield
        T = T + _mm(TL, T, na=terms[1], nb=terms[1])
        yield

    U = _mm(T, X, na=terms[4], nb=terms[4])
    yield
    uv = jnp.concatenate([U, v_s], axis=0)
    Y = ms[2 * C:] + _mm(jnp.concatenate([Grb, Grk], axis=1), uv, na=terms[5], nb=terms[5])
    y_ref[0, :, sl] = Y[:C] + Y[C:]
    s_scr[z, hp] = S * jnp.exp(tot) + _mm(uv, bk_end, dims=((0,), (0,)), na=terms[6], nb=terms[6])


def _scan_kernel(rf, vf, af, lwf, kmf, bbf, rb, vb, ab, lwb, kmb, bbb, s0_ref, yf_ref, yb_ref, sN_ref, s_scr, *, terms):
    c = pl.program_id(1)

    @pl.when(c == 0)
    def _():
        s_scr[...] = s0_ref[:, 0]

    n_pairs = s_scr.shape[1]
    chains = [_scan_chain(0, (rf, vf, af, lwf, kmf, bbf), hp, s_scr, yf_ref, terms) for hp in range(n_pairs)]
    chains += [_scan_chain(1, (rb, vb, ab, lwb, kmb, bbb), hp, s_scr, yb_ref, terms) for hp in range(n_pairs)]
    for _ in zip(*chains):
        pass
    for ch in chains:
        for _ in ch:
            pass

    @pl.when(c == pl.num_programs(1) - 1)
    def _():
        sN_ref[:, 0] = s_scr[...]


SCAN_TERMS = (1, 1, 1, 1, 2, 1, 1)


def _rwkv_scan(r, v, nkk, lw, km, bb, s0, terms=SCAN_TERMS):
    B, T, W = r.shape
    C = CHUNK
    NC = T // C
    NP = W // LANES
    fwd = lambda b, c: (b, c, 0)
    bwd = lambda b, c: (b, NC - 1 - c, 0)
    sh_f, sh_b = pl.BlockSpec((1, C, W), fwd), pl.BlockSpec((1, C, W), bwd)
    pd_f = pl.BlockSpec((1, 1, C, W), lambda b, c: (0, b, c, 0))
    pd_b = pl.BlockSpec((1, 1, C, W), lambda b, c: (1, b, NC - 1 - c, 0))
    state = pl.BlockSpec((2, 1, NP, LANES, LANES), lambda b, c: (0, b, 0, 0, 0))
    tok = jax.ShapeDtypeStruct((B, T, W), F32)
    return pl.pallas_call(
        functools.partial(_scan_kernel, terms=terms),
        grid=(B, NC),
        in_specs=[sh_f, sh_f, sh_f, pd_f, pd_f, pd_f, sh_b, sh_b, sh_b, pd_b, pd_b, pd_b, state],
        out_specs=[sh_f, sh_b, state],
        out_shape=[tok, tok, jax.ShapeDtypeStruct(s0.shape, F32)],
        scratch_shapes=[pltpu.VMEM((2, NP, LANES, LANES), F32)],
        compiler_params=_params(("parallel", "arbitrary")),
        name="rwkv_scan",
    )(r, v, nkk, lw, km, bb, r, v, nkk, lw, km, bb, s0)


def _rwkv_out_kernel(yf_ref, yb_ref, bv_ref, g_ref, lnw_ref, lnb_ref, ones_ref, o_ref):
    y = yf_ref[0] + yb_ref[0]
    ones_bd = ones_ref[...]
    inv_n = 1.0 / RWKV_HEAD
    mu = _segsum64(y, ones_bd) * inv_n
    d = y - mu
    var = _segsum64(d * d, ones_bd) * inv_n
    yn = d * lax.rsqrt(var + GN_EPS) * lnw_ref[...] + lnb_ref[...]
    o_ref[0] = ((yn + bv_ref[0]) * g_ref[0]).astype(o_ref.dtype)


def _rwkv_out(yf, yb, bv, g, ln_w, ln_b):
    B, T, W = yf.shape
    TT = _tile(T, 512, SUBLANES)
    ones_bd = _blockdiag2(jnp.ones((2, RWKV_HEAD, RWKV_HEAD), F32))
    s1 = pl.BlockSpec((1, TT, W), lambda b, i: (b, i, 0))
    return pl.pallas_call(
        _rwkv_out_kernel,
        grid=(B, T // TT),
        in_specs=[s1, s1, s1, s1,
                  pl.BlockSpec((1, W), lambda b, i: (0, 0)), pl.BlockSpec((1, W), lambda b, i: (0, 0)),
                  pl.BlockSpec((LANES, LANES), lambda b, i: (0, 0))],
        out_specs=s1,
        out_shape=jax.ShapeDtypeStruct((B, T, W), BF16),
        compiler_params=_params(("parallel", "parallel")),
        name="rwkv_out",
    )(yf, yb, bv, g, ln_w.reshape(1, W), ln_b.reshape(1, W), ones_bd)


def _merge_kernel(ao_ref, ro_ref, woa_ref, wor_ref, ga_ref, gr_ref, o_ref):
    ta = jnp.dot(ao_ref[...], woa_ref[...], preferred_element_type=F32)
    tr = jnp.dot(ro_ref[...], wor_ref[...], preferred_element_type=F32)
    o_ref[...] = (ga_ref[...].astype(F32) * ta + gr_ref[...].astype(F32) * tr).astype(o_ref.dtype)


def _merge(attn_o, rwkv_o, w_oa, w_or, gates):
    N, DA = attn_o.shape
    DR = rwkv_o.shape[1]
    D = w_oa.shape[1]
    TM = _tile(N, 1024, 16)
    TN = _tile(D, 512, LANES)
    nj = D // TN
    vmem = 2 * (TM * (DA + DR) * 2 + (DA + DR) * TN * 2 + 3 * TM * TN * 2) + 4 * TM * TN * 4
    return pl.pallas_call(
        _merge_kernel,
        grid=(N // TM, nj),
        in_specs=[pl.BlockSpec((TM, DA), lambda i, j: (i, 0)),
                  pl.BlockSpec((TM, DR), lambda i, j: (i, 0)),
                  pl.BlockSpec((DA, TN), lambda i, j: (0, j)),
                  pl.BlockSpec((DR, TN), lambda i, j: (0, j)),
                  pl.BlockSpec((TM, TN), lambda i, j: (i, j)),
                  pl.BlockSpec((TM, TN), lambda i, j: (i, nj + j))],
        out_specs=pl.BlockSpec((TM, TN), lambda i, j: (i, j)),
        out_shape=jax.ShapeDtypeStruct((N, D), BF16),
        compiler_params=_params(("parallel", "parallel"), vmem),
        name="merge",
    )(attn_o, rwkv_o, w_oa, w_or, gates, gates)


def _outproj_kernel(t_ref, w_ref, x_ref, mod_ref, o_ref, *, mod_base):
    acc = jnp.dot(t_ref[...], w_ref[...], preferred_element_type=F32)
    o_ref[...] = x_ref[...] + mod_ref[0, mod_base:mod_base + 1, :] * acc


def _outproj(t, w, x2, modtab, mod_row, mod_base):
    N, K = t.shape
    D = w.shape[1]
    seq, mod_row = mod_row
    TM = _tile(seq, 1024, 16)
    TN = _tile(D, 512, LANES)
    vmem = 2 * (TM * K * 2 + K * TN * 2 + 2 * TM * TN * 4) + 2 * TM * TN * 4
    return pl.pallas_call(
        functools.partial(_outproj_kernel, mod_base=mod_base),
        grid=(N // TM, D // TN),
        in_specs=[pl.BlockSpec((TM, K), lambda i, j: (i, 0)),
                  pl.BlockSpec((K, TN), lambda i, j: (0, j)),
                  pl.BlockSpec((TM, TN), lambda i, j: (i, j)),
                  pl.BlockSpec((1, N_MOD, TN), lambda i, j: (mod_row(i, TM), 0, j))],
        out_specs=pl.BlockSpec((TM, TN), lambda i, j: (i, j)),
        out_shape=jax.ShapeDtypeStruct((N, D), F32),
        compiler_params=_params(("parallel", "parallel"), vmem),
        name="outproj",
    )(t, w, x2, modtab)


def kernel(x, c, ctx, c_ctx, w_mod, b_mod, norm_ffn1, ffn1_w_in, ffn1_w_out, norm_mix, w_in, q_norm, k_norm, rwkv_mu, w0, w2, a0, a2, g2, k_k, k_a, r_k, ln_x_w, ln_x_b, w_oa, w_or, w_out, norm_ffn2, ffn2_w_in, ffn2_w_out, norm_final):
    B, S, D = x.shape
    CT = ctx.shape[1]
    assert w_mod.shape[0] == 1, "single layer"
    ATT_Q, ATT_KV = N_Q_HEADS * HEAD_DIM, N_KV_HEADS * HEAD_DIM
    assert w_in.shape[2] == ATT_Q + 2 * ATT_KV + RWKV_IN_W + 2 * D
    assert S % CHUNK == 0 and CT % CHUNK == 0 and S % GRID_W == 0

    lat_row = (S, lambda i, tm: (i * tm) // S)
    ctx_row = (B * CT, lambda i, tm: B)
    bf = lambda w: w.astype(BF16)

    modtab = _mod_table(c, c_ctx, w_mod[0], b_mod[0])

    x2 = x.reshape(B * S, D)
    cx2 = ctx.reshape(B * CT, D)
    w1_in, w1_out = bf(ffn1_w_in[0]), bf(ffn1_w_out[0])
    x2 = _ffn(x2, modtab, lat_row, 0, norm_ffn1[0], w1_in, w1_out)
    cx2 = _ffn(cx2, modtab, ctx_row, 0, norm_ffn1[0], w1_in, w1_out)

    wi = w_in[0]
    o = 0
    wq = bf(wi[:, o:o + ATT_Q]); o += ATT_Q
    wk = bf(wi[:, o:o + ATT_KV]); o += ATT_KV
    wv = bf(wi[:, o:o + ATT_KV]); o += ATT_KV
    wr = bf(wi[:, o:o + RWKV_IN_W]); o += RWKV_IN_W
    wg = bf(wi[:, o:])
    cos_t, sin_t = _rope_tables(S)
    qn, kn = q_norm[0].reshape(1, HEAD_DIM), k_norm[0].reshape(1, HEAD_DIM)
    hn_spec = lambda tm: pl.BlockSpec((1, HEAD_DIM), lambda i, j: (0, 0))
    rope_spec = lambda tm: pl.BlockSpec((tm, HEAD_DIM), lambda i, j: (i % (S // tm), 0))
    nm = norm_mix[0]
    q = _proj(x2, modtab, lat_row, 3, nm, wq, "headnorm_rope", BF16, (qn, cos_t, sin_t),
              (hn_spec, rope_spec, rope_spec), out_scale=HEAD_DIM ** -0.5 * LOG2E)
    k = _proj(x2, modtab, lat_row, 3, nm, wk, "headnorm_rope", BF16, (kn, cos_t, sin_t), (hn_spec, rope_spec, rope_spec))
    v = _proj(x2, modtab, lat_row, 3, nm, wv, "plain", BF16)
    p_rw = _proj(x2, modtab, lat_row, 3, nm, wr, "plain", F32)
    gates = _proj(x2, modtab, lat_row, 3, nm, wg, "sigmoid", BF16)
    kc = _proj(cx2, modtab, ctx_row, 3, nm, wk, "headnorm", BF16, (kn,), (hn_spec,))
    vc = _proj(cx2, modtab, ctx_row, 3, nm, wv, "plain", BF16)
    p_rw_c = _proj(cx2, modtab, ctx_row, 3, nm, wr, "plain", F32)

    k_all = jnp.concatenate([k.reshape(B, S, ATT_KV), kc.reshape(B, CT, ATT_KV)], axis=1)
    v_all = jnp.concatenate([v.reshape(B, S, ATT_KV), vc.reshape(B, CT, ATT_KV)], axis=1)
    attn_o = _attention(q.reshape(B, S, ATT_Q), k_all, v_all).reshape(B * S, ATT_Q)

    rw = (rwkv_mu[0], w0[0], w2[0], a0[0], a2[0], g2[0], k_k[0], k_a[0], r_k[0])
    rc, vcr, akc, lwc, kmc, bbc, _, _ = _rwkv_prep(p_rw_c.reshape(B, CT, RWKV_IN_W), *rw)
    s0 = jnp.zeros((2, B, RWKV_W // LANES, LANES, LANES), F32)
    _, _, s_ctx = _rwkv_scan(rc, vcr, akc, lwc, kmc, bbc, s0)
    rl, vl, akl, lwl, kml, bbl, gl, bvl = _rwkv_prep(p_rw.reshape(B, S, RWKV_IN_W), *rw)
    yf, yb, _ = _rwkv_scan(rl, vl, akl, lwl, kml, bbl, s_ctx)
    rwkv_o = _rwkv_out(yf, yb, bvl, gl, ln_x_w[0], ln_x_b[0]).reshape(B * S, RWKV_W)

    t = _merge(attn_o, rwkv_o, bf(w_oa[0]), bf(w_or[0]), gates)
    x2 = _outproj(t, bf(w_out[0]), x2, modtab, lat_row, 5)

    out = _ffn(x2, modtab, lat_row, 6, norm_ffn2[0], bf(ffn2_w_in[0]), bf(ffn2_w_out[0]), g_final=norm_final)
    return out.reshape(B, S, D)
```

```python
import functools
import math

import jax
import jax.numpy as jnp
from jax import lax
from jax.experimental import pallas as pl
from jax.experimental.pallas import tpu as pltpu

F32 = jnp.float32
BF16 = jnp.bfloat16

N_Q_HEADS = 16
N_KV_HEADS = 4
HEAD_DIM = 128
Q_PER_KV = N_Q_HEADS // N_KV_HEADS
RWKV_HEADS = 16
RWKV_HEAD = 64
RWKV_W = RWKV_HEADS * RWKV_HEAD
W_LORA = 64
A_LORA = 64
G_LORA = 128
RWKV_IN_W = 3 * RWKV_W + 2 * W_LORA + 2 * A_LORA + G_LORA
GRID_W = 64
ROPE_THETA = 10000.0
N_MOD = 9
EPS = 1e-6
GN_EPS = 64e-5
LOG2E = math.log2(math.e)
DECAY_RATE = math.exp(-0.5)

LANES = 128
SUBLANES = 8
VMEM_BYTES_V7X = 64 * 1024 * 1024
VMEM_LIMIT = 60000 * 1024

CHUNK = 64
PAIR = 2 * RWKV_HEAD
assert PAIR == LANES


def _tile(n, pref, mult):
    if n <= pref:
        return n
    best = None
    for t in range(mult, pref + 1, mult):
        if n % t == 0:
            best = t
    assert best is not None, (n, pref, mult)
    return best


def _params(sem, vmem=None):
    return pltpu.CompilerParams(dimension_semantics=sem, vmem_limit_bytes=min(vmem or VMEM_LIMIT, VMEM_LIMIT))


def _split_bf16(x, n):
    terms, rem = [], x
    for _ in range(n):
        t = rem.astype(BF16)
        terms.append(t)
        rem = rem - t.astype(F32)
    return terms


def _dotg(a, b, dims):
    return lax.dot_general(a, b, (dims, ((), ())), preferred_element_type=F32)


def _mm(a, b, dims=((1,), (0,)), na=1, nb=1):
    at = _split_bf16(a, na) if a.dtype != BF16 else [a]
    bt = _split_bf16(b, nb) if b.dtype != BF16 else [b]
    order = max(len(at), len(bt))
    acc = None
    for i, x in enumerate(at):
        for j, y in enumerate(bt):
            if i + j < order:
                d = _dotg(x, y, dims)
                acc = d if acc is None else acc + d
    return acc


def _rmsnorm(x, g):
    ms = jnp.mean(x * x, axis=-1, keepdims=True)
    return x * lax.rsqrt(ms + EPS) * g


def _mod_kernel(c_ref, w_ref, b_ref, o_ref):
    c = c_ref[...]
    s = (c * jax.nn.sigmoid(c)).astype(BF16)
    o_ref[...] = jnp.dot(s, w_ref[...].astype(BF16), preferred_element_type=F32) + b_ref[...]


def _mod_table(c, c_ctx, w_mod, b_mod):
    B, D = c.shape
    NO = w_mod.shape[1]
    rows = -(-(B + 1) // SUBLANES) * SUBLANES
    cin = jnp.zeros((rows, D), F32).at[:B].set(c).at[B].set(c_ctx)
    TN = _tile(NO, 1024, LANES)
    out = pl.pallas_call(
        _mod_kernel,
        grid=(NO // TN,),
        in_specs=[pl.BlockSpec((rows, D), lambda j: (0, 0)),
                  pl.BlockSpec((D, TN), lambda j: (0, j)),
                  pl.BlockSpec((1, TN), lambda j: (0, j))],
        out_specs=pl.BlockSpec((rows, TN), lambda j: (0, j)),
        out_shape=jax.ShapeDtypeStruct((rows, NO), F32),
        compiler_params=_params(("parallel",)),
        name="mod_table",
    )(cin, w_mod, b_mod.reshape(1, NO))
    return out[:B + 1].reshape(B + 1, N_MOD, D)


def _ffn_kernel(x_ref, mod_ref, g_ref, wg_ref, wu_ref, wo_ref, *rest, mod_base, final_norm, next_base):
    gf_ref = gn_ref = hn_ref = None
    if final_norm:
        gf_ref, o_ref, h_scr, acc_scr = rest
    elif next_base is not None:
        gn_ref, o_ref, hn_ref, h_scr, acc_scr = rest
    else:
        o_ref, h_scr, acc_scr = rest
    f = pl.program_id(1)

    @pl.when(f == 0)
    def _():
        shift = mod_ref[0, mod_base:mod_base + 1, :]
        scale = mod_ref[0, mod_base + 1:mod_base + 2, :]
        h = _rmsnorm(x_ref[...], g_ref[...]) * (1.0 + scale) + shift
        h_scr[...] = h.astype(BF16)
        acc_scr[...] = jnp.zeros_like(acc_scr)

    h = h_scr[...]
    gt = jnp.dot(h, wg_ref[...], preferred_element_type=F32)
    up = jnp.dot(h, wu_ref[...], preferred_element_type=F32)
    act = (gt * jax.nn.sigmoid(gt) * up).astype(BF16)
    acc_scr[...] += jnp.dot(act, wo_ref[...], preferred_element_type=F32)

    @pl.when(f == pl.num_programs(1) - 1)
    def _():
        gate = mod_ref[0, mod_base + 2:mod_base + 3, :]
        out = x_ref[...] + 0.5 * gate * acc_scr[...]
        if final_norm:
            out = _rmsnorm(out, gf_ref[...])
        o_ref[...] = out
        if next_base is not None:
            shift = mod_ref[0, next_base:next_base + 1, :]
            scale = mod_ref[0, next_base + 1:next_base + 2, :]
            hn_ref[...] = (_rmsnorm(out, gn_ref[...]) * (1.0 + scale) + shift).astype(BF16)


def _ffn(x2, modtab, mod_row, mod_base, g, w_in, w_out, g_final=None, next_norm=None):
    N, D = x2.shape
    F = w_out.shape[0]
    seq, mod_row = mod_row
    TM = _tile(seq, 512, SUBLANES)
    TF = _tile(F, 512, LANES)
    nf = F // TF
    in_specs = [pl.BlockSpec((TM, D), lambda i, f: (i, 0)),
                pl.BlockSpec((1, N_MOD, D), lambda i, f: (mod_row(i, TM), 0, 0)),
                pl.BlockSpec((1, D), lambda i, f: (0, 0)),
                pl.BlockSpec((D, TF), lambda i, f: (0, f)),
                pl.BlockSpec((D, TF), lambda i, f: (0, nf + f)),
                pl.BlockSpec((TF, D), lambda i, f: (f, 0))]
    args = [x2, modtab, g.reshape(1, D), w_in, w_in, w_out]
    assert g_final is None or next_norm is None
    row_spec = pl.BlockSpec((TM, D), lambda i, f: (i, 0))
    out_specs, out_shape = row_spec, jax.ShapeDtypeStruct((N, D), F32)
    if g_final is not None or next_norm is not None:
        in_specs.append(pl.BlockSpec((1, D), lambda i, f: (0, 0)))
        args.append((g_final if g_final is not None else next_norm[0]).reshape(1, D))
    if next_norm is not None:
        out_specs, out_shape = [row_spec, row_spec], [out_shape, jax.ShapeDtypeStruct((N, D), BF16)]
    vmem = 2 * (2 * TM * D * 4 + TM * D * 2 + 3 * D * TF * 2) + TM * D * 6 + 4 * TM * TF * 4
    return pl.pallas_call(
        functools.partial(_ffn_kernel, mod_base=mod_base, final_norm=g_final is not None,
                          next_base=None if next_norm is None else next_norm[1]),
        grid=(N // TM, nf),
        in_specs=in_specs,
        out_specs=out_specs,
        out_shape=out_shape,
        scratch_shapes=[pltpu.VMEM((TM, D), BF16), pltpu.VMEM((TM, D), F32)],
        compiler_params=_params(("parallel", "arbitrary"), vmem + (8 << 20)),
        name="ffn",
    )(*args)


def _swap32(t):
    lane = lax.broadcasted_iota(jnp.int32, t.shape, t.ndim - 1)
    return jnp.where((lane % 64) >= 32, pltpu.roll(t, 32, t.ndim - 1), pltpu.roll(t, LANES - 32, t.ndim - 1))


PROJ_TN = 512


def _proj_kernel(h_ref, w_ref, *rest, segs, rope):
    n_extra = len(segs) + (2 if rope else 0)
    extras, outs = rest[:n_extra], rest[n_extra:]
    j = pl.program_id(1)
    TM = h_ref.shape[0]
    n_split = 2 if TM % 32 == 0 else 1
    t0 = 0
    for si, (kind, n_tiles, scale) in enumerate(segs):
        def tile(kind=kind, scale=scale, o_ref=outs[si], hn_ref=extras[si]):
            for rows in (pl.ds(r * (TM // n_split), TM // n_split) for r in range(n_split)):
                acc = jnp.dot(h_ref[rows, :], w_ref[...], preferred_element_type=F32)
                if kind == "plain":
                    o_ref[rows, :] = acc.astype(o_ref.dtype)
                elif kind == "sigmoid":
                    o_ref[rows, :] = (0.5 * jnp.tanh(0.5 * acc) + 0.5).astype(o_ref.dtype)
                else:
                    ones = jnp.ones((HEAD_DIM, HEAD_DIM), BF16)
                    for hd in range(acc.shape[1] // HEAD_DIM):
                        t = acc[:, hd * HEAD_DIM:(hd + 1) * HEAD_DIM]
                        ms = _mm(t * t, ones, na=2, nb=1) * (1.0 / HEAD_DIM)
                        t = t * lax.rsqrt(ms + EPS) * hn_ref[...]
                        if kind == "headnorm_rope":
                            t = t * extras[-2][rows, :] + _swap32(t) * extras[-1][rows, :]
                        o_ref[rows, hd * HEAD_DIM:(hd + 1) * HEAD_DIM] = (t * scale).astype(o_ref.dtype)
        pl.when((j >= t0) & (j < t0 + n_tiles))(tile)
        t0 += n_tiles


def _proj(h2, w, seq, segs, rope_tables=None):
    N, D = h2.shape
    TN = PROJ_TN
    TM = _tile(seq, 1024, 16)
    tiles = [-(-width // TN) for _, width, _, _, _ in segs]
    assert w.shape[1] == TN * sum(tiles)
    dummy = jnp.zeros((1, HEAD_DIM), F32)
    extras = [dummy if hn is None else hn.reshape(1, HEAD_DIM) for _, _, _, hn, _ in segs]
    in_specs = [pl.BlockSpec((TM, D), lambda i, j: (i, 0)), pl.BlockSpec((D, TN), lambda i, j: (0, j))]
    in_specs += [pl.BlockSpec((1, HEAD_DIM), lambda i, j: (0, 0))] * len(segs)
    if rope_tables is not None:
        extras += list(rope_tables)
        in_specs += [pl.BlockSpec((TM, HEAD_DIM), lambda i, j: (i % (seq // TM), 0))] * 2
    out_specs, out_shape, t0 = [], [], 0
    for (kind, width, dtype, _, _), nt in zip(segs, tiles):
        out_specs.append(pl.BlockSpec((TM, TN), lambda i, j, t0=t0, nt=nt: (i, jnp.clip(j - t0, 0, nt - 1))))
        out_shape.append(jax.ShapeDtypeStruct((N, nt * TN), dtype))
        t0 += nt
    vmem = 2 * (TM * D * 2 + D * TN * 2 + len(segs) * TM * TN * 4 + 2 * TM * LANES * 4) + 6 * TM * TN * 4
    return pl.pallas_call(
        functools.partial(_proj_kernel, segs=tuple((k, nt, sc) for (k, _, _, _, sc), nt in zip(segs, tiles)),
                          rope=rope_tables is not None),
        grid=(N // TM, sum(tiles)),
        in_specs=in_specs,
        out_specs=out_specs,
        out_shape=out_shape,
        compiler_params=_params(("parallel", "arbitrary"), vmem),
        name="proj",
    )(h2, w, *extras)


def _proj_weight(w_cols):
    pad = lambda w: jnp.pad(w.astype(BF16), ((0, 0), (0, -w.shape[1] % PROJ_TN)))
    return jnp.concatenate([pad(w) for w in w_cols], axis=1)


def _rope_tables(n_tok):
    rows = n_tok // GRID_W
    row = jnp.repeat(jnp.arange(rows, dtype=jnp.int32), GRID_W).astype(F32)
    col = jnp.tile(jnp.arange(GRID_W, dtype=jnp.int32), rows).astype(F32)
    axis_dim = HEAD_DIM // 2
    inv = ROPE_THETA ** (-jnp.arange(0, axis_dim, 2, dtype=F32) / axis_dim)
    ang_r = row[:, None] * inv[None]
    ang_c = col[:, None] * inv[None]
    cr, sr, cc, sc = jnp.cos(ang_r), jnp.sin(ang_r), jnp.cos(ang_c), jnp.sin(ang_c)
    return (jnp.concatenate([cr, cr, cc, cc], axis=-1), jnp.concatenate([-sr, sr, -sc, sc], axis=-1))


def _attn_kernel(q_ref, k_ref, v_ref, o_ref, *, TQ, TK):
    n_k = k_ref.shape[1] // TK
    q = jnp.concatenate([q_ref[0, :, h * HEAD_DIM:(h + 1) * HEAD_DIM] for h in range(Q_PER_KV)], axis=0)
    R = q.shape[0]
    scores = lambda j: _dotg(q, k_ref[0, j * TK:(j + 1) * TK, :], ((1,), (1,)))
    m = jnp.full((R, 1), -jnp.inf, F32)
    acc = jnp.zeros((R, 2 * HEAD_DIM), F32)
    s_next = scores(0)
    for j in range(n_k):
        s = s_next
        if j + 1 < n_k:
            s_next = scores(j + 1)
        m_new = jnp.maximum(m, jnp.max(s, axis=-1, keepdims=True))
        p = jnp.exp2(s - m_new).astype(BF16)
        acc = jnp.exp2(m - m_new) * acc + jnp.dot(p, v_ref[0, j * TK:(j + 1) * TK, :], preferred_element_type=F32)
        m = m_new
    o = acc[:, :HEAD_DIM] / acc[:, HEAD_DIM:]
    for h in range(Q_PER_KV):
        o_ref[0, :, h * HEAD_DIM:(h + 1) * HEAD_DIM] = o[h * TQ:(h + 1) * TQ, :].astype(o_ref.dtype)


def _attention(q, k, v):
    B, T, _ = q.shape
    Tk = k.shape[1]
    v = jnp.concatenate([v.reshape(B, Tk, N_KV_HEADS, HEAD_DIM), jnp.ones((B, Tk, N_KV_HEADS, HEAD_DIM), v.dtype)],
                        axis=-1).reshape(B, Tk, N_KV_HEADS * 2 * HEAD_DIM)
    TQ = _tile(T, 256, 16)
    TK = _tile(Tk, 768, LANES)
    GW = Q_PER_KV * HEAD_DIM
    R = Q_PER_KV * TQ
    vmem = 2 * (2 * TQ * GW * 2 + 2 * Tk * HEAD_DIM * 2) + R * HEAD_DIM * (2 + 4 * 3) + 10 * R * TK * 4
    return pl.pallas_call(
        functools.partial(_attn_kernel, TQ=TQ, TK=TK),
        grid=(B, N_KV_HEADS, T // TQ),
        in_specs=[pl.BlockSpec((1, TQ, GW), lambda b, g, qi: (b, qi, g)),
                  pl.BlockSpec((1, Tk, HEAD_DIM), lambda b, g, qi: (b, 0, g)),
                  pl.BlockSpec((1, Tk, 2 * HEAD_DIM), lambda b, g, qi: (b, 0, g))],
        out_specs=pl.BlockSpec((1, TQ, GW), lambda b, g, qi: (b, qi, g)),
        out_shape=jax.ShapeDtypeStruct(q.shape, BF16),
        compiler_params=_params(("parallel", "parallel", "arbitrary"), vmem),
        name="gqa_flash",
    )(q, k, v)


def _segsum64(x, ones_bd):
    cols = []
    for c in range(x.shape[1] // LANES):
        cols.append(_mm(x[:, c * LANES:(c + 1) * LANES], ones_bd, na=3, nb=1))
    return jnp.concatenate(cols, axis=1)


def _prep_kernel(p_ref, pprev_ref, pnext_ref, mu_ref, w2_ref, w0_ref, a2_ref, a0_ref, g2_ref, kk_ref, ka_ref, rk_ref,
                 ones_ref, r_o, v_o, nkk_o, lw_o, km_o, bb_o, g_o, bv_o):
    p = p_ref[0]
    TT = p.shape[0]
    row = lax.broadcasted_iota(jnp.int32, (TT, 1), 0)
    i = pl.program_id(1)
    before = jnp.where(i == 0, 0.0, pprev_ref[0, SUBLANES - 1:SUBLANES, :])
    after = jnp.where(i == pl.num_programs(1) - 1, 0.0, pnext_ref[0, 0:1, :])
    prev = jnp.where(row == 0, before, pltpu.roll(p, 1, 0))
    nxt = jnp.where(row == TT - 1, after, pltpu.roll(p, TT - 1, 0))
    ps = p + mu_ref[...] * (0.5 * (prev + nxt) - p)
    W = RWKV_W
    r, k, v = ps[:, 0:W], ps[:, W:2 * W], ps[:, 2 * W:3 * W]
    lw = ps[:, 3 * W:3 * W + 2 * W_LORA]
    la = ps[:, 3 * W + 2 * W_LORA:3 * W + 2 * W_LORA + 2 * A_LORA]
    lg = ps[:, 3 * W + 2 * W_LORA + 2 * A_LORA:]
    ones_bd = ones_ref[...]
    w_raw = _mm(jnp.tanh(lw), w2_ref[...], na=2, nb=2) + w0_ref[...]
    logw = -DECAY_RATE * jax.nn.sigmoid(w_raw)
    a = jax.nn.sigmoid(_mm(la, a2_ref[...], na=2, nb=2) + a0_ref[...])
    g = _mm(jax.nn.sigmoid(lg), g2_ref[...], na=2, nb=2)
    kkv = k * kk_ref[...]
    kk = kkv / jnp.maximum(jnp.sqrt(_segsum64(kkv * kkv, ones_bd)), 1e-12)
    ka = ka_ref[...]
    kmod_sum = None
    for z in range(2):
        az = a[:, z * W:(z + 1) * W]
        kmod = k * (1.0 + (az - 1.0) * ka)
        lw_o[z, 0] = logw[:, z * W:(z + 1) * W]
        km_o[z, 0] = kmod
        bb_o[z, 0] = kk * az
        kmod_sum = kmod if kmod_sum is None else kmod_sum + kmod
    bonus = _segsum64(r * kmod_sum * rk_ref[...], ones_bd)
    r_o[0] = r
    v_o[0] = v
    nkk_o[0] = -kk
    g_o[0] = g
    bv_o[0] = bonus * v


def _blockdiag2(m):
    _, R, C = m.shape
    z = jnp.zeros((R, C), m.dtype)
    return jnp.concatenate([jnp.concatenate([m[0], z], axis=1), jnp.concatenate([z, m[1]], axis=1)], axis=0)


def _rwkv_prep(p, mu, w0, w2, a0, a2, g2, k_k, k_a, r_k):
    B, T, _ = p.shape
    WI = RWKV_IN_W
    TT = _tile(T, 256, SUBLANES)
    nT = T // TT
    W = RWKV_W
    ones_bd = _blockdiag2(jnp.ones((2, RWKV_HEAD, RWKV_HEAD), F32))
    full = lambda a: pl.BlockSpec(a.shape, lambda b, i: (0,) * a.ndim)
    w2bd, a2bd = _blockdiag2(w2), _blockdiag2(a2)
    ins = [p, p, p, mu.reshape(1, WI), w2bd, w0.reshape(1, 2 * W), a2bd, a0.reshape(1, 2 * W), g2,
           k_k.reshape(1, W), k_a.reshape(1, W), r_k.reshape(1, W), ones_bd]
    rb = TT // SUBLANES
    in_specs = [pl.BlockSpec((1, TT, WI), lambda b, i: (b, i, 0)),
                pl.BlockSpec((1, SUBLANES, WI), lambda b, i: (b, jnp.maximum(i * rb - 1, 0), 0)),
                pl.BlockSpec((1, SUBLANES, WI), lambda b, i: (b, jnp.minimum((i + 1) * rb, nT * rb - 1), 0))]
    in_specs += [full(a) for a in ins[3:]]
    tok = jax.ShapeDtypeStruct((B, T, W), F32)
    tokz = jax.ShapeDtypeStruct((2, B, T, W), F32)
    s1 = pl.BlockSpec((1, TT, W), lambda b, i: (b, i, 0))
    s2 = pl.BlockSpec((2, 1, TT, W), lambda b, i: (0, b, i, 0))
    return pl.pallas_call(
        _prep_kernel,
        grid=(B, nT),
        in_specs=in_specs,
        out_specs=[s1, s1, s1, s2, s2, s2, s1, s1],
        out_shape=[tok, tok, tok, tokz, tokz, tokz, tok, tok],
        compiler_params=_params(("parallel", "parallel")),
        name="rwkv_prep",
    )(*ins)


def _sm(x, lo):
    return jnp.concatenate([jnp.where(lo, x, 0.0), jnp.where(lo, 0.0, x)], axis=0)


def _scan_chain(z, refs, hp, s_scr, y_ref, terms):
    r_ref, v_ref, a_ref, lw_ref, km_ref, bb_ref = refs
    C = CHUNK
    sl = slice(hp * LANES, (hp + 1) * LANES)
    sgn = 1 - 2 * z
    ri = lax.broadcasted_iota(jnp.int32, (2 * C, 2 * C), 0)
    ci = lax.broadcasted_iota(jnp.int32, (2 * C, 2 * C), 1)
    strict = (ri - ci) * sgn > 0
    incl = (ri - ci) * sgn >= 0
    tr = lax.broadcasted_iota(jnp.int32, (C, C), 0)
    tc = lax.broadcasted_iota(jnp.int32, (C, C), 1)
    tri = jnp.where((tr - tc) * sgn >= 0, 1.0, 0.0).astype(BF16)
    eye = jnp.where(ri == ci, 1.0, 0.0)
    lo = lax.broadcasted_iota(jnp.int32, (C, LANES), 1) < RWKV_HEAD

    lw = lw_ref[0, 0, :, sl]
    cum = _mm(tri, lw, na=1, nb=3)
    yield
    tot = jnp.sum(lw, axis=0, keepdims=True)
    dec_in = jnp.exp(cum - lw)
    dec_out = jnp.exp(cum)
    inv = jnp.exp(-cum)
    rest = jnp.exp(tot - cum)
    a_t = _sm(a_ref[0, :, sl] * dec_in, lo)
    r_t = _sm(r_ref[0, :, sl] * dec_out, lo)
    bb = bb_ref[0, 0, :, sl]
    km = km_ref[0, 0, :, sl]
    b_t = _sm(bb * inv, lo)
    k_t = _sm(km * inv, lo)
    bk_end = jnp.concatenate([_sm(bb * rest, lo), _sm(km * rest, lo)], axis=0)
    v_s = _sm(v_ref[0, :, sl], lo)
    ar = jnp.concatenate([a_t, r_t], axis=0)
    G = _mm(ar, jnp.concatenate([b_t, k_t], axis=0), dims=((1,), (1,)), na=terms[0], nb=terms[0])
    S = s_scr[z, hp]
    ms = _mm(ar, S, dims=((1,), (1,)), na=terms[2], nb=terms[2])
    yield
    L = jnp.where(strict, G[:2 * C, :2 * C], 0.0)
    Lak = jnp.where(strict, G[:2 * C, 2 * C:], 0.0)
    Grb = jnp.where(incl, G[2 * C:, :2 * C], 0.0)
    Grk = jnp.where(incl, G[2 * C:, 2 * C:], 0.0)
    X = ms[:2 * C] + _mm(Lak, v_s, na=terms[3], nb=terms[3])

    T = eye + jnp.where((ri >> 1) == (ci >> 1), L, 0.0)
    for lvl in range(1, int(math.log2(C))):
        off = ((ri >> (lvl + 1)) == (ci >> (lvl + 1))) & ((ri >> lvl) != (ci >> lvl))
        TL = _mm(T, jnp.where(off, L, 0.0), na=terms[1], nb=terms[1])
        yield
        T = T + _mm(TL, T, na=terms[1], nb=terms[1])
        yield

    U = _mm(T, X, na=terms[4], nb=terms[4])
    yield
    uv = jnp.concatenate([U, v_s], axis=0)
    Y = ms[2 * C:] + _mm(jnp.concatenate([Grb, Grk], axis=1), uv, na=terms[5], nb=terms[5])
    y_ref[0, :, sl] = Y[:C] + Y[C:]
    s_scr[z, hp] = S * jnp.exp(tot) + _mm(uv, bk_end, dims=((0,), (0,)), na=terms[6], nb=terms[6])


def _scan_kernel(rf, vf, af, lwf, kmf, bbf, rb, vb, ab, lwb, kmb, bbb, s0_ref, yf_ref, yb_ref, sN_ref, s_scr, *, terms):
    c = pl.program_id(1)

    @pl.when(c == 0)
    def _():
        s_scr[...] = s0_ref[:, 0]

    n_pairs = s_scr.shape[1]
    chains = [_scan_chain(0, (rf, vf, af, lwf, kmf, bbf), hp, s_scr, yf_ref, terms) for hp in range(n_pairs)]
    chains += [_scan_chain(1, (rb, vb, ab, lwb, kmb, bbb), hp, s_scr, yb_ref, terms) for hp in range(n_pairs)]
    for _ in zip(*chains):
        pass
    for ch in chains:
        for _ in ch:
            pass

    @pl.when(c == pl.num_programs(1) - 1)
    def _():
        sN_ref[:, 0] = s_scr[...]


SCAN_TERMS = (1, 1, 1, 1, 2, 1, 1)


def _rwkv_scan(r, v, nkk, lw, km, bb, s0, terms=SCAN_TERMS):
    B, T, W = r.shape
    C = CHUNK
    NC = T // C
    NP = W // LANES
    fwd = lambda b, c: (b, c, 0)
    bwd = lambda b, c: (b, NC - 1 - c, 0)
    sh_f, sh_b = pl.BlockSpec((1, C, W), fwd), pl.BlockSpec((1, C, W), bwd)
    pd_f = pl.BlockSpec((1, 1, C, W), lambda b, c: (0, b, c, 0))
    pd_b = pl.BlockSpec((1, 1, C, W), lambda b, c: (1, b, NC - 1 - c, 0))
    state = pl.BlockSpec((2, 1, NP, LANES, LANES), lambda b, c: (0, b, 0, 0, 0))
    tok = jax.ShapeDtypeStruct((B, T, W), F32)
    return pl.pallas_call(
        functools.partial(_scan_kernel, terms=terms),
        grid=(B, NC),
        in_specs=[sh_f, sh_f, sh_f, pd_f, pd_f, pd_f, sh_b, sh_b, sh_b, pd_b, pd_b, pd_b, state],
        out_specs=[sh_f, sh_b, state],
        out_shape=[tok, tok, jax.ShapeDtypeStruct(s0.shape, F32)],
        scratch_shapes=[pltpu.VMEM((2, NP, LANES, LANES), F32)],
        compiler_params=_params(("parallel", "arbitrary")),
        name="rwkv_scan",
    )(r, v, nkk, lw, km, bb, r, v, nkk, lw, km, bb, s0)


def _merge_kernel(ao_ref, yf_ref, yb_ref, bv_ref, g_ref, lnw_ref, lnb_ref, ones_ref, woa_ref, wor_ref, ga_ref, gr_ref,
                  o_ref, ro_scr):
    @pl.when(pl.program_id(1) == 0)
    def _():
        y = yf_ref[...] + yb_ref[...]
        ones_bd = ones_ref[...]
        inv_n = 1.0 / RWKV_HEAD
        mu = _segsum64(y, ones_bd) * inv_n
        d = y - mu
        var = _segsum64(d * d, ones_bd) * inv_n
        yn = d * lax.rsqrt(var + GN_EPS) * lnw_ref[...] + lnb_ref[...]
        ro_scr[...] = ((yn + bv_ref[...]) * g_ref[...]).astype(BF16)

    ta = jnp.dot(ao_ref[...], woa_ref[...], preferred_element_type=F32)
    tr = jnp.dot(ro_scr[...], wor_ref[...], preferred_element_type=F32)
    o_ref[...] = (ga_ref[...].astype(F32) * ta + gr_ref[...].astype(F32) * tr).astype(o_ref.dtype)


def _merge(attn_o, yf, yb, bv, g, ln_w, ln_b, w_oa, w_or, gates):
    N, DA = attn_o.shape
    DR = yf.shape[1]
    D = w_oa.shape[1]
    TM = _tile(N, 512, 16)
    TN = _tile(D, 512, LANES)
    nj = D // TN
    ones_bd = _blockdiag2(jnp.ones((2, RWKV_HEAD, RWKV_HEAD), F32))
    tokr = pl.BlockSpec((TM, DR), lambda i, j: (i, 0))
    vec = pl.BlockSpec((1, DR), lambda i, j: (0, 0))
    vmem = 2 * (TM * DA * 2 + 4 * TM * DR * 4 + (DA + DR) * TN * 2 + 3 * TM * TN * 2) + TM * DR * 2 + 6 * TM * DR * 4
    return pl.pallas_call(
        _merge_kernel,
        grid=(N // TM, nj),
        in_specs=[pl.BlockSpec((TM, DA), lambda i, j: (i, 0)), tokr, tokr, tokr, tokr, vec, vec,
                  pl.BlockSpec((LANES, LANES), lambda i, j: (0, 0)),
                  pl.BlockSpec((DA, TN), lambda i, j: (0, j)),
                  pl.BlockSpec((DR, TN), lambda i, j: (0, j)),
                  pl.BlockSpec((TM, TN), lambda i, j: (i, j)),
                  pl.BlockSpec((TM, TN), lambda i, j: (i, nj + j))],
        out_specs=pl.BlockSpec((TM, TN), lambda i, j: (i, j)),
        out_shape=jax.ShapeDtypeStruct((N, D), BF16),
        scratch_shapes=[pltpu.VMEM((TM, DR), BF16)],
        compiler_params=_params(("parallel", "arbitrary"), vmem),
        name="merge",
    )(attn_o, yf, yb, bv, g, ln_w.reshape(1, DR), ln_b.reshape(1, DR), ones_bd, w_oa, w_or, gates, gates)


def _outproj_kernel(t_ref, w_ref, x_ref, mod_ref, o_ref, *, mod_base):
    acc = jnp.dot(t_ref[...], w_ref[...], preferred_element_type=F32)
    o_ref[...] = x_ref[...] + mod_ref[0, mod_base:mod_base + 1, :] * acc


def _outproj(t, w, x2, modtab, mod_row, mod_base):
    N, K = t.shape
    D = w.shape[1]
    seq, mod_row = mod_row
    TM = _tile(seq, 1024, 16)
    TN = _tile(D, 512, LANES)
    vmem = 2 * (TM * K * 2 + K * TN * 2 + 2 * TM * TN * 4) + 2 * TM * TN * 4
    return pl.pallas_call(
        functools.partial(_outproj_kernel, mod_base=mod_base),
        grid=(N // TM, D // TN),
        in_specs=[pl.BlockSpec((TM, K), lambda i, j: (i, 0)),
                  pl.BlockSpec((K, TN), lambda i, j: (0, j)),
                  pl.BlockSpec((TM, TN), lambda i, j: (i, j)),
                  pl.BlockSpec((1, N_MOD, TN), lambda i, j: (mod_row(i, TM), 0, j))],
        out_specs=pl.BlockSpec((TM, TN), lambda i, j: (i, j)),
        out_shape=jax.ShapeDtypeStruct((N, D), F32),
        compiler_params=_params(("parallel", "parallel"), vmem),
        name="outproj",
    )(t, w, x2, modtab)


def kernel(x, c, ctx, c_ctx, w_mod, b_mod, norm_ffn1, ffn1_w_in, ffn1_w_out, norm_mix, w_in, q_norm, k_norm, rwkv_mu, w0, w2, a0, a2, g2, k_k, k_a, r_k, ln_x_w, ln_x_b, w_oa, w_or, w_out, norm_ffn2, ffn2_w_in, ffn2_w_out, norm_final):
    B, S, D = x.shape
    CT = ctx.shape[1]
    assert w_mod.shape[0] == 1, "single layer"
    ATT_Q, ATT_KV = N_Q_HEADS * HEAD_DIM, N_KV_HEADS * HEAD_DIM
    assert w_in.shape[2] == ATT_Q + 2 * ATT_KV + RWKV_IN_W + 2 * D
    assert S % CHUNK == 0 and CT % CHUNK == 0 and S % GRID_W == 0

    lat_row = (S, lambda i, tm: (i * tm) // S)
    ctx_row = (B * CT, lambda i, tm: B)
    bf = lambda w: w.astype(BF16)

    modtab = _mod_table(c, c_ctx, w_mod[0], b_mod[0])

    x2 = x.reshape(B * S, D)
    cx2 = ctx.reshape(B * CT, D)
    w1_in, w1_out = bf(ffn1_w_in[0]), bf(ffn1_w_out[0])
    x2, h2 = _ffn(x2, modtab, lat_row, 0, norm_ffn1[0], w1_in, w1_out, next_norm=(norm_mix[0], 3))
    _, hc2 = _ffn(cx2, modtab, ctx_row, 0, norm_ffn1[0], w1_in, w1_out, next_norm=(norm_mix[0], 3))

    wi = w_in[0]
    o = 0
    wq = wi[:, o:o + ATT_Q]; o += ATT_Q
    wk = wi[:, o:o + ATT_KV]; o += ATT_KV
    wv = wi[:, o:o + ATT_KV]; o += ATT_KV
    wr = wi[:, o:o + RWKV_IN_W]; o += RWKV_IN_W
    wg = wi[:, o:]
    q_seg = ("headnorm_rope", ATT_Q, BF16, q_norm[0], HEAD_DIM ** -0.5 * LOG2E)
    v_seg = ("plain", ATT_KV, BF16, None, 1.0)
    r_seg = ("plain", RWKV_IN_W, F32, None, 1.0)
    g_seg = ("sigmoid", 2 * D, BF16, None, 1.0)
    q, k, v, p_rw, gates = _proj(h2, _proj_weight([wq, wk, wv, wr, wg]), S,
                                 [q_seg, ("headnorm_rope", ATT_KV, BF16, k_norm[0], 1.0), v_seg, r_seg, g_seg],
                                 rope_tables=_rope_tables(S))
    kc, vc, p_rw_c = _proj(hc2, _proj_weight([wk, wv, wr]), B * CT,
                           [("headnorm", ATT_KV, BF16, k_norm[0], 1.0), v_seg, r_seg])

    k_all = jnp.concatenate([k.reshape(B, S, ATT_KV), kc.reshape(B, CT, ATT_KV)], axis=1)
    v_all = jnp.concatenate([v.reshape(B, S, ATT_KV), vc.reshape(B, CT, ATT_KV)], axis=1)
    attn_o = _attention(q.reshape(B, S, ATT_Q), k_all, v_all).reshape(B * S, ATT_Q)

    rw = (rwkv_mu[0], w0[0], w2[0], a0[0], a2[0], g2[0], k_k[0], k_a[0], r_k[0])
    rc, vcr, akc, lwc, kmc, bbc, _, _ = _rwkv_prep(p_rw_c.reshape(B, CT, -1), *rw)
    s0 = jnp.zeros((2, B, RWKV_W // LANES, LANES, LANES), F32)
    _, _, s_ctx = _rwkv_scan(rc, vcr, akc, lwc, kmc, bbc, s0)
    rl, vl, akl, lwl, kml, bbl, gl, bvl = _rwkv_prep(p_rw.reshape(B, S, -1), *rw)
    yf, yb, _ = _rwkv_scan(rl, vl, akl, lwl, kml, bbl, s_ctx)
    flat = lambda a: a.reshape(B * S, RWKV_W)
    t = _merge(attn_o, flat(yf), flat(yb), flat(bvl), flat(gl), ln_x_w[0], ln_x_b[0], bf(w_oa[0]), bf(w_or[0]), gates)
    x2 = _outproj(t, bf(w_out[0]), x2, modtab, lat_row, 5)

    out = _ffn(x2, modtab, lat_row, 6, norm_ffn2[0], bf(ffn2_w_in[0]), bf(ffn2_w_out[0]), g_final=norm_final)
    return out.reshape(B, S, D)
```

```python
import functools
import math

import jax
import jax.numpy as jnp
from jax import lax
from jax.experimental import pallas as pl
from jax.experimental.pallas import tpu as pltpu

F32 = jnp.float32
BF16 = jnp.bfloat16

N_Q_HEADS = 16
N_KV_HEADS = 4
HEAD_DIM = 128
Q_PER_KV = N_Q_HEADS // N_KV_HEADS
RWKV_HEADS = 16
RWKV_HEAD = 64
RWKV_W = RWKV_HEADS * RWKV_HEAD
W_LORA = 64
A_LORA = 64
G_LORA = 128
RWKV_IN_W = 3 * RWKV_W + 2 * W_LORA + 2 * A_LORA + G_LORA
GRID_W = 64
ROPE_THETA = 10000.0
N_MOD = 9
EPS = 1e-6
GN_EPS = 64e-5
LOG2E = math.log2(math.e)
DECAY_RATE = math.exp(-0.5)

LANES = 128
SUBLANES = 8
VMEM_BYTES_V7X = 64 * 1024 * 1024
VMEM_LIMIT = 60000 * 1024

CHUNK = 64
PAIR = 2 * RWKV_HEAD
assert PAIR == LANES


def _tile(n, pref, mult):
    if n <= pref:
        return n
    best = None
    for t in range(mult, pref + 1, mult):
        if n % t == 0:
            best = t
    assert best is not None, (n, pref, mult)
    return best


def _params(sem, vmem=None):
    return pltpu.CompilerParams(dimension_semantics=sem, vmem_limit_bytes=min(vmem or VMEM_LIMIT, VMEM_LIMIT))


def _split_bf16(x, n):
    terms, rem = [], x
    for _ in range(n):
        t = rem.astype(BF16)
        terms.append(t)
        rem = rem - t.astype(F32)
    return terms


def _dotg(a, b, dims):
    return lax.dot_general(a, b, (dims, ((), ())), preferred_element_type=F32)


def _mm(a, b, dims=((1,), (0,)), na=1, nb=1):
    at = _split_bf16(a, na) if a.dtype != BF16 else [a]
    bt = _split_bf16(b, nb) if b.dtype != BF16 else [b]
    order = max(len(at), len(bt))
    acc = None
    for i, x in enumerate(at):
        for j, y in enumerate(bt):
            if i + j < order:
                d = _dotg(x, y, dims)
                acc = d if acc is None else acc + d
    return acc


def _rmsnorm(x, g):
    ms = jnp.mean(x * x, axis=-1, keepdims=True)
    return x * lax.rsqrt(ms + EPS) * g


def _mod_kernel(c_ref, w_ref, b_ref, o_ref):
    c = c_ref[...]
    s = (c * jax.nn.sigmoid(c)).astype(BF16)
    o_ref[...] = jnp.dot(s, w_ref[...].astype(BF16), preferred_element_type=F32) + b_ref[...]


def _mod_table(c, c_ctx, w_mod, b_mod):
    B, D = c.shape
    NO = w_mod.shape[1]
    rows = -(-(B + 1) // SUBLANES) * SUBLANES
    cin = jnp.zeros((rows, D), F32).at[:B].set(c).at[B].set(c_ctx)
    TN = _tile(NO, 1024, LANES)
    out = pl.pallas_call(
        _mod_kernel,
        grid=(NO // TN,),
        in_specs=[pl.BlockSpec((rows, D), lambda j: (0, 0)),
                  pl.BlockSpec((D, TN), lambda j: (0, j)),
                  pl.BlockSpec((1, TN), lambda j: (0, j))],
        out_specs=pl.BlockSpec((rows, TN), lambda j: (0, j)),
        out_shape=jax.ShapeDtypeStruct((rows, NO), F32),
        compiler_params=_params(("parallel",)),
        name="mod_table",
    )(cin, w_mod, b_mod.reshape(1, NO))
    return out[:B + 1].reshape(B + 1, N_MOD, D)


def _ffn_kernel(x_ref, mod_ref, g_ref, wg_ref, wu_ref, wo_ref, *rest, mod_base, final_norm, next_base):
    gf_ref = gn_ref = hn_ref = None
    if final_norm:
        gf_ref, o_ref, h_scr, acc_scr = rest
    elif next_base is not None:
        gn_ref, o_ref, hn_ref, h_scr, acc_scr = rest
    else:
        o_ref, h_scr, acc_scr = rest
    f = pl.program_id(1)
    last = pl.num_programs(1) - 1
    TM = x_ref.shape[0]
    halves = [pl.ds(0, TM // 2), pl.ds(TM // 2, TM // 2)] if TM % 32 == 0 else [pl.ds(0, TM)]

    def prologue(rows):
        shift = mod_ref[0, mod_base:mod_base + 1, :]
        scale = mod_ref[0, mod_base + 1:mod_base + 2, :]
        h = _rmsnorm(x_ref[rows, :], g_ref[...]) * (1.0 + scale) + shift
        h_scr[rows, :] = h.astype(BF16)

    def chunk(rows, first):
        h = h_scr[rows, :]
        gt = jnp.dot(h, wg_ref[...], preferred_element_type=F32)
        up = jnp.dot(h, wu_ref[...], preferred_element_type=F32)
        act = (gt * jax.nn.sigmoid(gt) * up).astype(BF16)
        part = jnp.dot(act, wo_ref[...], preferred_element_type=F32)
        acc_scr[rows, :] = part if first else acc_scr[rows, :] + part

    def epilogue(rows):
        gate = mod_ref[0, mod_base + 2:mod_base + 3, :]
        out = x_ref[rows, :] + 0.5 * gate * acc_scr[rows, :]
        if final_norm:
            out = _rmsnorm(out, gf_ref[...])
        o_ref[rows, :] = out
        if next_base is not None:
            shift = mod_ref[0, next_base:next_base + 1, :]
            scale = mod_ref[0, next_base + 1:next_base + 2, :]
            hn_ref[rows, :] = (_rmsnorm(out, gn_ref[...]) * (1.0 + scale) + shift).astype(BF16)

    @pl.when(f == 0)
    def _():
        for rows in halves:
            prologue(rows)
            chunk(rows, first=True)

    @pl.when((f > 0) & (f < last))
    def _():
        chunk(pl.ds(0, TM), first=False)

    @pl.when(f == last)
    def _():
        for rows in halves:
            chunk(rows, first=False)
            epilogue(rows)


def _ffn(x2, modtab, mod_row, mod_base, g, w_in, w_out, g_final=None, next_norm=None):
    N, D = x2.shape
    F = w_out.shape[0]
    seq, mod_row = mod_row
    TM = _tile(seq, 512, SUBLANES)
    TF = _tile(F, 512, LANES)
    nf = F // TF
    assert nf >= 2, "first and last d_ff chunk are distinct grid steps"
    in_specs = [pl.BlockSpec((TM, D), lambda i, f: (i, 0)),
                pl.BlockSpec((1, N_MOD, D), lambda i, f: (mod_row(i, TM), 0, 0)),
                pl.BlockSpec((1, D), lambda i, f: (0, 0)),
                pl.BlockSpec((D, TF), lambda i, f: (0, f)),
                pl.BlockSpec((D, TF), lambda i, f: (0, nf + f)),
                pl.BlockSpec((TF, D), lambda i, f: (f, 0))]
    args = [x2, modtab, g.reshape(1, D), w_in, w_in, w_out]
    assert g_final is None or next_norm is None
    row_spec = pl.BlockSpec((TM, D), lambda i, f: (i, 0))
    out_specs, out_shape = row_spec, jax.ShapeDtypeStruct((N, D), F32)
    if g_final is not None or next_norm is not None:
        in_specs.append(pl.BlockSpec((1, D), lambda i, f: (0, 0)))
        args.append((g_final if g_final is not None else next_norm[0]).reshape(1, D))
    if next_norm is not None:
        out_specs, out_shape = [row_spec, row_spec], [out_shape, jax.ShapeDtypeStruct((N, D), BF16)]
    vmem = 2 * (2 * TM * D * 4 + TM * D * 2 + 3 * D * TF * 2) + TM * D * 6 + 4 * TM * TF * 4
    return pl.pallas_call(
        functools.partial(_ffn_kernel, mod_base=mod_base, final_norm=g_final is not None,
                          next_base=None if next_norm is None else next_norm[1]),
        grid=(N // TM, nf),
        in_specs=in_specs,
        out_specs=out_specs,
        out_shape=out_shape,
        scratch_shapes=[pltpu.VMEM((TM, D), BF16), pltpu.VMEM((TM, D), F32)],
        compiler_params=_params(("parallel", "arbitrary"), vmem + (8 << 20)),
        name="ffn",
    )(*args)


def _swap32(t):
    lane = lax.broadcasted_iota(jnp.int32, t.shape, t.ndim - 1)
    return jnp.where((lane % 64) >= 32, pltpu.roll(t, 32, t.ndim - 1), pltpu.roll(t, LANES - 32, t.ndim - 1))


PROJ_TN = 512


def _proj_kernel(h_ref, w_ref, *rest, segs, rope):
    n_extra = len(segs) + (2 if rope else 0)
    extras, outs = rest[:n_extra], rest[n_extra:]
    j = pl.program_id(1)
    TM = h_ref.shape[0]
    n_split = 2 if TM % 32 == 0 else 1
    t0 = 0
    for si, (kind, n_tiles, scale) in enumerate(segs):
        def tile(kind=kind, scale=scale, o_ref=outs[si], hn_ref=extras[si]):
            for rows in (pl.ds(r * (TM // n_split), TM // n_split) for r in range(n_split)):
                acc = jnp.dot(h_ref[rows, :], w_ref[...], preferred_element_type=F32)
                if kind == "plain":
                    o_ref[rows, :] = acc.astype(o_ref.dtype)
                elif kind == "sigmoid":
                    o_ref[rows, :] = (0.5 * jnp.tanh(0.5 * acc) + 0.5).astype(o_ref.dtype)
                else:
                    ones = jnp.ones((HEAD_DIM, HEAD_DIM), BF16)
                    for hd in range(acc.shape[1] // HEAD_DIM):
                        t = acc[:, hd * HEAD_DIM:(hd + 1) * HEAD_DIM]
                        ms = _mm(t * t, ones, na=2, nb=1) * (1.0 / HEAD_DIM)
                        t = t * lax.rsqrt(ms + EPS) * hn_ref[...]
                        if kind == "headnorm_rope":
                            t = t * extras[-2][rows, :] + _swap32(t) * extras[-1][rows, :]
                        o_ref[rows, hd * HEAD_DIM:(hd + 1) * HEAD_DIM] = (t * scale).astype(o_ref.dtype)
        pl.when((j >= t0) & (j < t0 + n_tiles))(tile)
        t0 += n_tiles


def _proj(h2, w, seq, segs, rope_tables=None):
    N, D = h2.shape
    TN = PROJ_TN
    TM = _tile(seq, 1024, 16)
    tiles = [-(-width // TN) for _, width, _, _, _ in segs]
    assert w.shape[1] == TN * sum(tiles)
    dummy = jnp.zeros((1, HEAD_DIM), F32)
    extras = [dummy if hn is None else hn.reshape(1, HEAD_DIM) for _, _, _, hn, _ in segs]
    in_specs = [pl.BlockSpec((TM, D), lambda i, j: (i, 0)), pl.BlockSpec((D, TN), lambda i, j: (0, j))]
    in_specs += [pl.BlockSpec((1, HEAD_DIM), lambda i, j: (0, 0))] * len(segs)
    if rope_tables is not None:
        extras += list(rope_tables)
        in_specs += [pl.BlockSpec((TM, HEAD_DIM), lambda i, j: (i % (seq // TM), 0))] * 2
    out_specs, out_shape, t0 = [], [], 0
    for (kind, width, dtype, _, _), nt in zip(segs, tiles):
        out_specs.append(pl.BlockSpec((TM, TN), lambda i, j, t0=t0, nt=nt: (i, jnp.clip(j - t0, 0, nt - 1))))
        out_shape.append(jax.ShapeDtypeStruct((N, nt * TN), dtype))
        t0 += nt
    vmem = 2 * (TM * D * 2 + D * TN * 2 + len(segs) * TM * TN * 4 + 2 * TM * LANES * 4) + 6 * TM * TN * 4
    return pl.pallas_call(
        functools.partial(_proj_kernel, segs=tuple((k, nt, sc) for (k, _, _, _, sc), nt in zip(segs, tiles)),
                          rope=rope_tables is not None),
        grid=(N // TM, sum(tiles)),
        in_specs=in_specs,
        out_specs=out_specs,
        out_shape=out_shape,
        compiler_params=_params(("parallel", "arbitrary"), vmem),
        name="proj",
    )(h2, w, *extras)


def _proj_weight(w_cols):
    pad = lambda w: jnp.pad(w.astype(BF16), ((0, 0), (0, -w.shape[1] % PROJ_TN)))
    return jnp.concatenate([pad(w) for w in w_cols], axis=1)


def _rope_tables(n_tok):
    rows = n_tok // GRID_W
    row = jnp.repeat(jnp.arange(rows, dtype=jnp.int32), GRID_W).astype(F32)
    col = jnp.tile(jnp.arange(GRID_W, dtype=jnp.int32), rows).astype(F32)
    axis_dim = HEAD_DIM // 2
    inv = ROPE_THETA ** (-jnp.arange(0, axis_dim, 2, dtype=F32) / axis_dim)
    ang_r = row[:, None] * inv[None]
    ang_c = col[:, None] * inv[None]
    cr, sr, cc, sc = jnp.cos(ang_r), jnp.sin(ang_r), jnp.cos(ang_c), jnp.sin(ang_c)
    return (jnp.concatenate([cr, cr, cc, cc], axis=-1), jnp.concatenate([-sr, sr, -sc, sc], axis=-1))


def _attn_kernel(q_ref, k_ref, v_ref, o_ref, *, TQ, TK):
    n_k = k_ref.shape[1] // TK
    q = jnp.concatenate([q_ref[0, :, h * HEAD_DIM:(h + 1) * HEAD_DIM] for h in range(Q_PER_KV)], axis=0)
    R = q.shape[0]
    scores = lambda j: _dotg(q, k_ref[0, j * TK:(j + 1) * TK, :], ((1,), (1,)))
    m = jnp.full((R, 1), -jnp.inf, F32)
    acc = jnp.zeros((R, 2 * HEAD_DIM), F32)
    s_next = scores(0)
    for j in range(n_k):
        s = s_next
        if j + 1 < n_k:
            s_next = scores(j + 1)
        m_new = jnp.maximum(m, jnp.max(s, axis=-1, keepdims=True))
        p = jnp.exp2(s - m_new).astype(BF16)
        acc = jnp.exp2(m - m_new) * acc + jnp.dot(p, v_ref[0, j * TK:(j + 1) * TK, :], preferred_element_type=F32)
        m = m_new
    o = acc[:, :HEAD_DIM] / acc[:, HEAD_DIM:]
    for h in range(Q_PER_KV):
        o_ref[0, :, h * HEAD_DIM:(h + 1) * HEAD_DIM] = o[h * TQ:(h + 1) * TQ, :].astype(o_ref.dtype)


def _attention(q, k, v):
    B, T, _ = q.shape
    Tk = k.shape[1]
    ones = jnp.ones((B, Tk, HEAD_DIM), v.dtype)
    v = jnp.concatenate([piece for g in range(N_KV_HEADS)
                         for piece in (v[:, :, g * HEAD_DIM:(g + 1) * HEAD_DIM], ones)], axis=-1)
    TQ = _tile(T, 256, 16)
    TK = _tile(Tk, 768, LANES)
    GW = Q_PER_KV * HEAD_DIM
    R = Q_PER_KV * TQ
    vmem = 2 * (2 * TQ * GW * 2 + 2 * Tk * HEAD_DIM * 2) + R * HEAD_DIM * (2 + 4 * 3) + 10 * R * TK * 4
    return pl.pallas_call(
        functools.partial(_attn_kernel, TQ=TQ, TK=TK),
        grid=(B, N_KV_HEADS, T // TQ),
        in_specs=[pl.BlockSpec((1, TQ, GW), lambda b, g, qi: (b, qi, g)),
                  pl.BlockSpec((1, Tk, HEAD_DIM), lambda b, g, qi: (b, 0, g)),
                  pl.BlockSpec((1, Tk, 2 * HEAD_DIM), lambda b, g, qi: (b, 0, g))],
        out_specs=pl.BlockSpec((1, TQ, GW), lambda b, g, qi: (b, qi, g)),
        out_shape=jax.ShapeDtypeStruct(q.shape, BF16),
        compiler_params=_params(("parallel", "parallel", "arbitrary"), vmem),
        name="gqa_flash",
    )(q, k, v)


def _segsum64(x, ones_bd):
    cols = []
    for c in range(x.shape[1] // LANES):
        cols.append(_mm(x[:, c * LANES:(c + 1) * LANES], ones_bd, na=3, nb=1))
    return jnp.concatenate(cols, axis=1)


def _prep_kernel(p_ref, pprev_ref, pnext_ref, mu_ref, w2_ref, w0_ref, a2_ref, a0_ref, g2_ref, kk_ref, ka_ref, rk_ref,
                 ones_ref, r_o, v_o, nkk_o, lw_o, km_o, bb_o, g_o, bv_o):
    p = p_ref[0]
    TT = p.shape[0]
    row = lax.broadcasted_iota(jnp.int32, (TT, 1), 0)
    i = pl.program_id(1)
    before = jnp.where(i == 0, 0.0, pprev_ref[0, SUBLANES - 1:SUBLANES, :])
    after = jnp.where(i == pl.num_programs(1) - 1, 0.0, pnext_ref[0, 0:1, :])
    prev = jnp.where(row == 0, before, pltpu.roll(p, 1, 0))
    nxt = jnp.where(row == TT - 1, after, pltpu.roll(p, TT - 1, 0))
    ps = p + mu_ref[...] * (0.5 * (prev + nxt) - p)
    W = RWKV_W
    r, k, v = ps[:, 0:W], ps[:, W:2 * W], ps[:, 2 * W:3 * W]
    lw = ps[:, 3 * W:3 * W + 2 * W_LORA]
    la = ps[:, 3 * W + 2 * W_LORA:3 * W + 2 * W_LORA + 2 * A_LORA]
    lg = ps[:, 3 * W + 2 * W_LORA + 2 * A_LORA:]
    ones_bd = ones_ref[...]
    w_raw = _mm(jnp.tanh(lw), w2_ref[...], na=2, nb=2) + w0_ref[...]
    logw = -DECAY_RATE * jax.nn.sigmoid(w_raw)
    a = jax.nn.sigmoid(_mm(la, a2_ref[...], na=2, nb=2) + a0_ref[...])
    g = _mm(jax.nn.sigmoid(lg), g2_ref[...], na=2, nb=2)
    kkv = k * kk_ref[...]
    kk = kkv / jnp.maximum(jnp.sqrt(_segsum64(kkv * kkv, ones_bd)), 1e-12)
    ka = ka_ref[...]
    kmod_sum = None
    for z in range(2):
        az = a[:, z * W:(z + 1) * W]
        kmod = k * (1.0 + (az - 1.0) * ka)
        lw_o[z, 0] = logw[:, z * W:(z + 1) * W]
        km_o[z, 0] = kmod
        bb_o[z, 0] = kk * az
        kmod_sum = kmod if kmod_sum is None else kmod_sum + kmod
    bonus = _segsum64(r * kmod_sum * rk_ref[...], ones_bd)
    r_o[0] = r
    v_o[0] = v
    nkk_o[0] = -kk
    g_o[0] = g
    bv_o[0] = bonus * v


def _blockdiag2(m):
    _, R, C = m.shape
    z = jnp.zeros((R, C), m.dtype)
    return jnp.concatenate([jnp.concatenate([m[0], z], axis=1), jnp.concatenate([z, m[1]], axis=1)], axis=0)


def _rwkv_prep(p, mu, w0, w2, a0, a2, g2, k_k, k_a, r_k):
    B, T, _ = p.shape
    WI = RWKV_IN_W
    TT = _tile(T, 256, SUBLANES)
    nT = T // TT
    W = RWKV_W
    ones_bd = _blockdiag2(jnp.ones((2, RWKV_HEAD, RWKV_HEAD), F32))
    full = lambda a: pl.BlockSpec(a.shape, lambda b, i: (0,) * a.ndim)
    w2bd, a2bd = _blockdiag2(w2), _blockdiag2(a2)
    ins = [p, p, p, mu.reshape(1, WI), w2bd, w0.reshape(1, 2 * W), a2bd, a0.reshape(1, 2 * W), g2,
           k_k.reshape(1, W), k_a.reshape(1, W), r_k.reshape(1, W), ones_bd]
    rb = TT // SUBLANES
    in_specs = [pl.BlockSpec((1, TT, WI), lambda b, i: (b, i, 0)),
                pl.BlockSpec((1, SUBLANES, WI), lambda b, i: (b, jnp.maximum(i * rb - 1, 0), 0)),
                pl.BlockSpec((1, SUBLANES, WI), lambda b, i: (b, jnp.minimum((i + 1) * rb, nT * rb - 1), 0))]
    in_specs += [full(a) for a in ins[3:]]
    tok = jax.ShapeDtypeStruct((B, T, W), F32)
    tokz = jax.ShapeDtypeStruct((2, B, T, W), F32)
    s1 = pl.BlockSpec((1, TT, W), lambda b, i: (b, i, 0))
    s2 = pl.BlockSpec((2, 1, TT, W), lambda b, i: (0, b, i, 0))
    return pl.pallas_call(
        _prep_kernel,
        grid=(B, nT),
        in_specs=in_specs,
        out_specs=[s1, s1, s1, s2, s2, s2, s1, s1],
        out_shape=[tok, tok, tok, tokz, tokz, tokz, tok, tok],
        compiler_params=_params(("parallel", "parallel")),
        name="rwkv_prep",
    )(*ins)


def _sm(x, lo):
    return jnp.concatenate([jnp.where(lo, x, 0.0), jnp.where(lo, 0.0, x)], axis=0)


def _scan_chain(z, refs, hp, s_scr, y_ref, terms):
    r_ref, v_ref, a_ref, lw_ref, km_ref, bb_ref = refs
    C = CHUNK
    sl = slice(hp * LANES, (hp + 1) * LANES)
    sgn = 1 - 2 * z
    ri = lax.broadcasted_iota(jnp.int32, (2 * C, 2 * C), 0)
    ci = lax.broadcasted_iota(jnp.int32, (2 * C, 2 * C), 1)
    strict = (ri - ci) * sgn > 0
    incl = (ri - ci) * sgn >= 0
    eye = jnp.where(ri == ci, 1.0, 0.0)
    lo = lax.broadcasted_iota(jnp.int32, (C, LANES), 1) < RWKV_HEAD
    row = lax.broadcasted_iota(jnp.int32, (C, LANES), 0)

    lw = lw_ref[0, 0, :, sl]
    cum = lw
    for k in range(int(math.log2(C))):
        d = 1 << k
        if z == 0:
            cum = cum + jnp.where(row >= d, pltpu.roll(cum, d, 0), 0.0)
        else:
            cum = cum + jnp.where(row < C - d, pltpu.roll(cum, C - d, 0), 0.0)
    tot = jnp.sum(lw, axis=0, keepdims=True)
    dec_in = jnp.exp(cum - lw)
    dec_out = jnp.exp(cum)
    inv = jnp.exp(-cum)
    rest = jnp.exp(tot - cum)
    a_t = _sm(a_ref[0, :, sl] * dec_in, lo)
    r_t = _sm(r_ref[0, :, sl] * dec_out, lo)
    bb = bb_ref[0, 0, :, sl]
    km = km_ref[0, 0, :, sl]
    b_t = _sm(bb * inv, lo)
    k_t = _sm(km * inv, lo)
    bk_end = jnp.concatenate([_sm(bb * rest, lo), _sm(km * rest, lo)], axis=0)
    v_s = _sm(v_ref[0, :, sl], lo)
    ar = jnp.concatenate([a_t, r_t], axis=0)
    G = _mm(ar, jnp.concatenate([b_t, k_t], axis=0), dims=((1,), (1,)), na=terms[0], nb=terms[0])
    S = s_scr[z, hp]
    ms = _mm(ar, S, dims=((1,), (1,)), na=terms[2], nb=terms[2])
    yield
    L = jnp.where(strict, G[:2 * C, :2 * C], 0.0)
    Lak = jnp.where(strict, G[:2 * C, 2 * C:], 0.0)
    Grb = jnp.where(incl, G[2 * C:, :2 * C], 0.0)
    Grk = jnp.where(incl, G[2 * C:, 2 * C:], 0.0)
    X = ms[:2 * C] + _mm(Lak, v_s, na=terms[3], nb=terms[3])

    T = eye + jnp.where((ri >> 1) == (ci >> 1), L, 0.0)
    for lvl in range(1, int(math.log2(C))):
        off = ((ri >> (lvl + 1)) == (ci >> (lvl + 1))) & ((ri >> lvl) != (ci >> lvl))
        TL = _mm(T, jnp.where(off, L, 0.0), na=terms[1], nb=terms[1])
        yield
        T = T + _mm(TL, T, na=terms[1], nb=terms[1])
        yield

    U = _mm(T, X, na=terms[4][0], nb=terms[4][1])
    yield
    uv = jnp.concatenate([U, v_s], axis=0)
    Y = ms[2 * C:] + _mm(jnp.concatenate([Grb, Grk], axis=1), uv, na=terms[5], nb=terms[5])
    y_ref[0, :, sl] = Y[:C] + Y[C:]
    s_scr[z, hp] = S * jnp.exp(tot) + _mm(uv, bk_end, dims=((0,), (0,)), na=terms[6], nb=terms[6])


def _scan_kernel(rf, vf, af, lwf, kmf, bbf, rb, vb, ab, lwb, kmb, bbb, s0_ref, yf_ref, yb_ref, sN_ref, s_scr, *, terms):
    c = pl.program_id(1)

    @pl.when(c == 0)
    def _():
        s_scr[...] = s0_ref[:, 0]

    n_pairs = s_scr.shape[1]
    chains = [_scan_chain(0, (rf, vf, af, lwf, kmf, bbf), hp, s_scr, yf_ref, terms) for hp in range(n_pairs)]
    chains += [_scan_chain(1, (rb, vb, ab, lwb, kmb, bbb), hp, s_scr, yb_ref, terms) for hp in range(n_pairs)]
    for _ in zip(*chains):
        pass
    for ch in chains:
        for _ in ch:
            pass

    @pl.when(c == pl.num_programs(1) - 1)
    def _():
        sN_ref[:, 0] = s_scr[...]


SCAN_TERMS = (1, 1, 1, 1, (1, 2), 1, 1)


def _rwkv_scan(r, v, nkk, lw, km, bb, s0, terms=SCAN_TERMS):
    B, T, W = r.shape
    C = CHUNK
    NC = T // C
    NP = W // LANES
    fwd = lambda b, c: (b, c, 0)
    bwd = lambda b, c: (b, NC - 1 - c, 0)
    sh_f, sh_b = pl.BlockSpec((1, C, W), fwd), pl.BlockSpec((1, C, W), bwd)
    pd_f = pl.BlockSpec((1, 1, C, W), lambda b, c: (0, b, c, 0))
    pd_b = pl.BlockSpec((1, 1, C, W), lambda b, c: (1, b, NC - 1 - c, 0))
    state = pl.BlockSpec((2, 1, NP, LANES, LANES), lambda b, c: (0, b, 0, 0, 0))
    tok = jax.ShapeDtypeStruct((B, T, W), F32)
    return pl.pallas_call(
        functools.partial(_scan_kernel, terms=terms),
        grid=(B, NC),
        in_specs=[sh_f, sh_f, sh_f, pd_f, pd_f, pd_f, sh_b, sh_b, sh_b, pd_b, pd_b, pd_b, state],
        out_specs=[sh_f, sh_b, state],
        out_shape=[tok, tok, jax.ShapeDtypeStruct(s0.shape, F32)],
        scratch_shapes=[pltpu.VMEM((2, NP, LANES, LANES), F32)],
        compiler_params=_params(("parallel", "arbitrary")),
        name="rwkv_scan",
    )(r, v, nkk, lw, km, bb, r, v, nkk, lw, km, bb, s0)


def _rwkv_out_kernel(yf_ref, yb_ref, bv_ref, g_ref, lnw_ref, lnb_ref, ones_ref, o_ref):
    y = yf_ref[0] + yb_ref[0]
    ones_bd = ones_ref[...]
    inv_n = 1.0 / RWKV_HEAD
    mu = _segsum64(y, ones_bd) * inv_n
    d = y - mu
    var = _segsum64(d * d, ones_bd) * inv_n
    yn = d * lax.rsqrt(var + GN_EPS) * lnw_ref[...] + lnb_ref[...]
    o_ref[0] = ((yn + bv_ref[0]) * g_ref[0]).astype(o_ref.dtype)


def _rwkv_out(yf, yb, bv, g, ln_w, ln_b):
    B, T, W = yf.shape
    TT = _tile(T, 512, SUBLANES)
    ones_bd = _blockdiag2(jnp.ones((2, RWKV_HEAD, RWKV_HEAD), F32))
    s1 = pl.BlockSpec((1, TT, W), lambda b, i: (b, i, 0))
    return pl.pallas_call(
        _rwkv_out_kernel,
        grid=(B, T // TT),
        in_specs=[s1, s1, s1, s1,
                  pl.BlockSpec((1, W), lambda b, i: (0, 0)), pl.BlockSpec((1, W), lambda b, i: (0, 0)),
                  pl.BlockSpec((LANES, LANES), lambda b, i: (0, 0))],
        out_specs=s1,
        out_shape=jax.ShapeDtypeStruct((B, T, W), BF16),
        compiler_params=_params(("parallel", "parallel")),
        name="rwkv_out",
    )(yf, yb, bv, g, ln_w.reshape(1, W), ln_b.reshape(1, W), ones_bd)


def _merge_kernel(ao_ref, ro_ref, woa_ref, wor_ref, ga_ref, gr_ref, o_ref):
    ta = jnp.dot(ao_ref[...], woa_ref[...], preferred_element_type=F32)
    tr = jnp.dot(ro_ref[...], wor_ref[...], preferred_element_type=F32)
    o_ref[...] = (ga_ref[...].astype(F32) * ta + gr_ref[...].astype(F32) * tr).astype(o_ref.dtype)


def _merge(attn_o, rwkv_o, w_oa, w_or, gates):
    N, DA = attn_o.shape
    DR = rwkv_o.shape[1]
    D = w_oa.shape[1]
    TM = _tile(N, 1024, 16)
    TN = _tile(D, 512, LANES)
    nj = D // TN
    vmem = 2 * (TM * (DA + DR) * 2 + (DA + DR) * TN * 2 + 3 * TM * TN * 2) + 4 * TM * TN * 4
    return pl.pallas_call(
        _merge_kernel,
        grid=(N // TM, nj),
        in_specs=[pl.BlockSpec((TM, DA), lambda i, j: (i, 0)),
                  pl.BlockSpec((TM, DR), lambda i, j: (i, 0)),
                  pl.BlockSpec((DA, TN), lambda i, j: (0, j)),
                  pl.BlockSpec((DR, TN), lambda i, j: (0, j)),
                  pl.BlockSpec((TM, TN), lambda i, j: (i, j)),
                  pl.BlockSpec((TM, TN), lambda i, j: (i, nj + j))],
        out_specs=pl.BlockSpec((TM, TN), lambda i, j: (i, j)),
        out_shape=jax.ShapeDtypeStruct((N, D), BF16),
        compiler_params=_params(("parallel", "parallel"), vmem),
        name="merge",
    )(attn_o, rwkv_o, w_oa, w_or, gates, gates)


def _outproj_kernel(t_ref, w_ref, x_ref, mod_ref, o_ref, *, mod_base):
    acc = jnp.dot(t_ref[...], w_ref[...], preferred_element_type=F32)
    o_ref[...] = x_ref[...] + mod_ref[0, mod_base:mod_base + 1, :] * acc


def _outproj(t, w, x2, modtab, mod_row, mod_base):
    N, K = t.shape
    D = w.shape[1]
    seq, mod_row = mod_row
    TM = _tile(seq, 1024, 16)
    TN = _tile(D, 512, LANES)
    vmem = 2 * (TM * K * 2 + K * TN * 2 + 2 * TM * TN * 4) + 2 * TM * TN * 4
    return pl.pallas_call(
        functools.partial(_outproj_kernel, mod_base=mod_base),
        grid=(N // TM, D // TN),
        in_specs=[pl.BlockSpec((TM, K), lambda i, j: (i, 0)),
                  pl.BlockSpec((K, TN), lambda i, j: (0, j)),
                  pl.BlockSpec((TM, TN), lambda i, j: (i, j)),
                  pl.BlockSpec((1, N_MOD, TN), lambda i, j: (mod_row(i, TM), 0, j))],
        out_specs=pl.BlockSpec((TM, TN), lambda i, j: (i, j)),
        out_shape=jax.ShapeDtypeStruct((N, D), F32),
        compiler_params=_params(("parallel", "parallel"), vmem),
        name="outproj",
    )(t, w, x2, modtab)


def kernel(x, c, ctx, c_ctx, w_mod, b_mod, norm_ffn1, ffn1_w_in, ffn1_w_out, norm_mix, w_in, q_norm, k_norm, rwkv_mu, w0, w2, a0, a2, g2, k_k, k_a, r_k, ln_x_w, ln_x_b, w_oa, w_or, w_out, norm_ffn2, ffn2_w_in, ffn2_w_out, norm_final):
    B, S, D = x.shape
    CT = ctx.shape[1]
    assert w_mod.shape[0] == 1, "single layer"
    ATT_Q, ATT_KV = N_Q_HEADS * HEAD_DIM, N_KV_HEADS * HEAD_DIM
    assert w_in.shape[2] == ATT_Q + 2 * ATT_KV + RWKV_IN_W + 2 * D
    assert S % CHUNK == 0 and CT % CHUNK == 0 and S % GRID_W == 0

    lat_row = (S, lambda i, tm: (i * tm) // S)
    ctx_row = (B * CT, lambda i, tm: B)
    bf = lambda w: w.astype(BF16)

    modtab = _mod_table(c, c_ctx, w_mod[0], b_mod[0])

    x2 = x.reshape(B * S, D)
    cx2 = ctx.reshape(B * CT, D)
    w1_in, w1_out = bf(ffn1_w_in[0]), bf(ffn1_w_out[0])
    x2, h2 = _ffn(x2, modtab, lat_row, 0, norm_ffn1[0], w1_in, w1_out, next_norm=(norm_mix[0], 3))
    _, hc2 = _ffn(cx2, modtab, ctx_row, 0, norm_ffn1[0], w1_in, w1_out, next_norm=(norm_mix[0], 3))

    wi = w_in[0]
    o = 0
    wq = wi[:, o:o + ATT_Q]; o += ATT_Q
    wk = wi[:, o:o + ATT_KV]; o += ATT_KV
    wv = wi[:, o:o + ATT_KV]; o += ATT_KV
    wr = wi[:, o:o + RWKV_IN_W]; o += RWKV_IN_W
    wg = wi[:, o:]
    q_seg = ("headnorm_rope", ATT_Q, BF16, q_norm[0], HEAD_DIM ** -0.5 * LOG2E)
    v_seg = ("plain", ATT_KV, BF16, None, 1.0)
    r_seg = ("plain", RWKV_IN_W, F32, None, 1.0)
    g_seg = ("sigmoid", 2 * D, BF16, None, 1.0)
    q, k, v, p_rw, gates = _proj(h2, _proj_weight([wq, wk, wv, wr, wg]), S,
                                 [q_seg, ("headnorm_rope", ATT_KV, BF16, k_norm[0], 1.0), v_seg, r_seg, g_seg],
                                 rope_tables=_rope_tables(S))
    kc, vc, p_rw_c = _proj(hc2, _proj_weight([wk, wv, wr]), B * CT,
                           [("headnorm", ATT_KV, BF16, k_norm[0], 1.0), v_seg, r_seg])

    k_all = jnp.concatenate([k.reshape(B, S, ATT_KV), kc.reshape(B, CT, ATT_KV)], axis=1)
    v_all = jnp.concatenate([v.reshape(B, S, ATT_KV), vc.reshape(B, CT, ATT_KV)], axis=1)
    attn_o = _attention(q.reshape(B, S, ATT_Q), k_all, v_all).reshape(B * S, ATT_Q)

    rw = (rwkv_mu[0], w0[0], w2[0], a0[0], a2[0], g2[0], k_k[0], k_a[0], r_k[0])
    rc, vcr, akc, lwc, kmc, bbc, _, _ = _rwkv_prep(p_rw_c.reshape(B, CT, -1), *rw)
    s0 = jnp.zeros((2, B, RWKV_W // LANES, LANES, LANES), F32)
    _, _, s_ctx = _rwkv_scan(rc, vcr, akc, lwc, kmc, bbc, s0)
    rl, vl, akl, lwl, kml, bbl, gl, bvl = _rwkv_prep(p_rw.reshape(B, S, -1), *rw)
    yf, yb, _ = _rwkv_scan(rl, vl, akl, lwl, kml, bbl, s_ctx)
    rwkv_o = _rwkv_out(yf, yb, bvl, gl, ln_x_w[0], ln_x_b[0]).reshape(B * S, RWKV_W)

    t = _merge(attn_o, rwkv_o, bf(w_oa[0]), bf(w_or[0]), gates)
    x2 = _outproj(t, bf(w_out[0]), x2, modtab, lat_row, 5)

    out = _ffn(x2, modtab, lat_row, 6, norm_ffn2[0], bf(ffn2_w_in[0]), bf(ffn2_w_out[0]), g_final=norm_final)
    return out.reshape(B, S, D)
```

```python
import functools
import math

import jax
import jax.numpy as jnp
from jax import lax
from jax.experimental import pallas as pl
from jax.experimental.pallas import tpu as pltpu

F32 = jnp.float32
BF16 = jnp.bfloat16

N_Q_HEADS = 16
N_KV_HEADS = 4
HEAD_DIM = 128
Q_PER_KV = N_Q_HEADS // N_KV_HEADS
RWKV_HEADS = 16
RWKV_HEAD = 64
RWKV_W = RWKV_HEADS * RWKV_HEAD
W_LORA = 64
A_LORA = 64
G_LORA = 128
RWKV_IN_W = 3 * RWKV_W + 2 * W_LORA + 2 * A_LORA + G_LORA
GRID_W = 64
ROPE_THETA = 10000.0
N_MOD = 9
EPS = 1e-6
GN_EPS = 64e-5
LOG2E = math.log2(math.e)
DECAY_RATE = math.exp(-0.5)

LANES = 128
SUBLANES = 8
VMEM_BYTES_V7X = 64 * 1024 * 1024
VMEM_LIMIT = 60000 * 1024

CHUNK = 64
PAIR = 2 * RWKV_HEAD
assert PAIR == LANES


def _tile(n, pref, mult):
    if n <= pref:
        return n
    best = None
    for t in range(mult, pref + 1, mult):
        if n % t == 0:
            best = t
    assert best is not None, (n, pref, mult)
    return best


def _params(sem, vmem=None):
    return pltpu.CompilerParams(dimension_semantics=sem, vmem_limit_bytes=min(vmem or VMEM_LIMIT, VMEM_LIMIT))


def _split_bf16(x, n):
    terms, rem = [], x
    for _ in range(n):
        t = rem.astype(BF16)
        terms.append(t)
        rem = rem - t.astype(F32)
    return terms


def _dotg(a, b, dims):
    return lax.dot_general(a, b, (dims, ((), ())), preferred_element_type=F32)


def _mm(a, b, dims=((1,), (0,)), na=1, nb=1):
    at = _split_bf16(a, na) if a.dtype != BF16 else [a]
    if isinstance(b, (list, tuple)):
        bt = list(b)
    else:
        bt = _split_bf16(b, nb) if b.dtype != BF16 else [b]
    order = max(len(at), len(bt))
    acc = None
    for i, x in enumerate(at):
        for j, y in enumerate(bt):
            if i + j < order:
                d = _dotg(x, y, dims)
                acc = d if acc is None else acc + d
    return acc


def _rmsnorm(x, g):
    ms = jnp.mean(x * x, axis=-1, keepdims=True)
    return x * lax.rsqrt(ms + EPS) * g


def _mod_kernel(c_ref, w_ref, b_ref, o_ref):
    c = c_ref[...]
    s = (c * jax.nn.sigmoid(c)).astype(BF16)
    o_ref[...] = jnp.dot(s, w_ref[...].astype(BF16), preferred_element_type=F32) + b_ref[...]


def _mod_table(c, c_ctx, w_mod, b_mod):
    B, D = c.shape
    NO = w_mod.shape[1]
    rows = -(-(B + 1) // SUBLANES) * SUBLANES
    cin = jnp.zeros((rows, D), F32).at[:B].set(c).at[B].set(c_ctx)
    TN = _tile(NO, 1024, LANES)
    out = pl.pallas_call(
        _mod_kernel,
        grid=(NO // TN,),
        in_specs=[pl.BlockSpec((rows, D), lambda j: (0, 0)),
                  pl.BlockSpec((D, TN), lambda j: (0, j)),
                  pl.BlockSpec((1, TN), lambda j: (0, j))],
        out_specs=pl.BlockSpec((rows, TN), lambda j: (0, j)),
        out_shape=jax.ShapeDtypeStruct((rows, NO), F32),
        compiler_params=_params(("parallel",)),
        name="mod_table",
    )(cin, w_mod, b_mod.reshape(1, NO))
    return out[:B + 1].reshape(B + 1, N_MOD, D)


def _ffn_kernel(x_ref, mod_ref, g_ref, wg_ref, wu_ref, wo_ref, *rest, mod_base, final_norm, next_base):
    gf_ref = gn_ref = hn_ref = None
    if final_norm:
        gf_ref, o_ref, h_scr, acc_scr = rest
    elif next_base is not None:
        gn_ref, o_ref, hn_ref, h_scr, acc_scr = rest
    else:
        o_ref, h_scr, acc_scr = rest
    f = pl.program_id(1)
    last = pl.num_programs(1) - 1
    TM = x_ref.shape[0]
    halves = [pl.ds(0, TM // 2), pl.ds(TM // 2, TM // 2)] if TM % 32 == 0 else [pl.ds(0, TM)]

    def prologue(rows):
        shift = mod_ref[0, mod_base:mod_base + 1, :]
        scale = mod_ref[0, mod_base + 1:mod_base + 2, :]
        h = _rmsnorm(x_ref[rows, :], g_ref[...]) * (1.0 + scale) + shift
        h_scr[rows, :] = h.astype(BF16)

    def chunk(rows, first):
        h = h_scr[rows, :]
        gt = jnp.dot(h, wg_ref[...], preferred_element_type=F32)
        up = jnp.dot(h, wu_ref[...], preferred_element_type=F32)
        act = (gt * jax.nn.sigmoid(gt) * up).astype(BF16)
        part = jnp.dot(act, wo_ref[...], preferred_element_type=F32)
        acc_scr[rows, :] = part if first else acc_scr[rows, :] + part

    def epilogue(rows):
        gate = mod_ref[0, mod_base + 2:mod_base + 3, :]
        out = x_ref[rows, :] + 0.5 * gate * acc_scr[rows, :]
        if final_norm:
            out = _rmsnorm(out, gf_ref[...])
        o_ref[rows, :] = out
        if next_base is not None:
            shift = mod_ref[0, next_base:next_base + 1, :]
            scale = mod_ref[0, next_base + 1:next_base + 2, :]
            hn_ref[rows, :] = (_rmsnorm(out, gn_ref[...]) * (1.0 + scale) + shift).astype(BF16)

    @pl.when(f == 0)
    def _():
        for rows in halves:
            prologue(rows)
            chunk(rows, first=True)

    @pl.when((f > 0) & (f < last))
    def _():
        chunk(pl.ds(0, TM), first=False)

    @pl.when(f == last)
    def _():
        for rows in halves:
            chunk(rows, first=False)
            epilogue(rows)


def _ffn(x2, modtab, mod_row, mod_base, g, w_in, w_out, g_final=None, next_norm=None):
    N, D = x2.shape
    F = w_out.shape[0]
    seq, mod_row = mod_row
    TM = _tile(seq, 512, SUBLANES)
    TF = _tile(F, 512, LANES)
    nf = F // TF
    assert nf >= 2, "first and last d_ff chunk are distinct grid steps"
    in_specs = [pl.BlockSpec((TM, D), lambda i, f: (i, 0)),
                pl.BlockSpec((1, N_MOD, D), lambda i, f: (mod_row(i, TM), 0, 0)),
                pl.BlockSpec((1, D), lambda i, f: (0, 0)),
                pl.BlockSpec((D, TF), lambda i, f: (0, f)),
                pl.BlockSpec((D, TF), lambda i, f: (0, nf + f)),
                pl.BlockSpec((TF, D), lambda i, f: (f, 0))]
    args = [x2, modtab, g.reshape(1, D), w_in, w_in, w_out]
    assert g_final is None or next_norm is None
    row_spec = pl.BlockSpec((TM, D), lambda i, f: (i, 0))
    out_specs, out_shape = row_spec, jax.ShapeDtypeStruct((N, D), F32)
    if g_final is not None or next_norm is not None:
        in_specs.append(pl.BlockSpec((1, D), lambda i, f: (0, 0)))
        args.append((g_final if g_final is not None else next_norm[0]).reshape(1, D))
    if next_norm is not None:
        out_specs, out_shape = [row_spec, row_spec], [out_shape, jax.ShapeDtypeStruct((N, D), BF16)]
    vmem = 2 * (2 * TM * D * 4 + TM * D * 2 + 3 * D * TF * 2) + TM * D * 6 + 4 * TM * TF * 4
    return pl.pallas_call(
        functools.partial(_ffn_kernel, mod_base=mod_base, final_norm=g_final is not None,
                          next_base=None if next_norm is None else next_norm[1]),
        grid=(N // TM, nf),
        in_specs=in_specs,
        out_specs=out_specs,
        out_shape=out_shape,
        scratch_shapes=[pltpu.VMEM((TM, D), BF16), pltpu.VMEM((TM, D), F32)],
        compiler_params=_params(("parallel", "arbitrary"), vmem + (8 << 20)),
        name="ffn",
    )(*args)


def _swap32(t):
    lane = lax.broadcasted_iota(jnp.int32, t.shape, t.ndim - 1)
    return jnp.where((lane % 64) >= 32, pltpu.roll(t, 32, t.ndim - 1), pltpu.roll(t, LANES - 32, t.ndim - 1))


PROJ_TN = 512


def _proj_kernel(h_ref, w_ref, *rest, segs, rope):
    n_extra = len(segs) + (2 if rope else 0)
    extras, outs = rest[:n_extra], rest[n_extra:]
    j = pl.program_id(1)
    TM = h_ref.shape[0]
    n_split = 2 if TM % 32 == 0 else 1
    t0 = 0
    for si, (kind, n_tiles, scale) in enumerate(segs):
        def tile(kind=kind, scale=scale, o_ref=outs[si], hn_ref=extras[si]):
            for rows in (pl.ds(r * (TM // n_split), TM // n_split) for r in range(n_split)):
                acc = jnp.dot(h_ref[rows, :], w_ref[...], preferred_element_type=F32)
                if kind == "plain":
                    o_ref[rows, :] = acc.astype(o_ref.dtype)
                elif kind == "sigmoid":
                    o_ref[rows, :] = (0.5 * jnp.tanh(0.5 * acc) + 0.5).astype(o_ref.dtype)
                else:
                    ones = jnp.ones((HEAD_DIM, HEAD_DIM), BF16)
                    for hd in range(acc.shape[1] // HEAD_DIM):
                        t = acc[:, hd * HEAD_DIM:(hd + 1) * HEAD_DIM]
                        ms = _mm(t * t, ones, na=2, nb=1) * (1.0 / HEAD_DIM)
                        t = t * lax.rsqrt(ms + EPS) * hn_ref[...]
                        if kind == "headnorm_rope":
                            t = t * extras[-2][rows, :] + _swap32(t) * extras[-1][rows, :]
                        o_ref[rows, hd * HEAD_DIM:(hd + 1) * HEAD_DIM] = (t * scale).astype(o_ref.dtype)
        pl.when((j >= t0) & (j < t0 + n_tiles))(tile)
        t0 += n_tiles


def _proj(h2, w, seq, segs, rope_tables=None):
    N, D = h2.shape
    TN = PROJ_TN
    TM = _tile(seq, 1024, 16)
    tiles = [-(-width // TN) for _, width, _, _, _ in segs]
    assert w.shape[1] == TN * sum(tiles)
    dummy = jnp.zeros((1, HEAD_DIM), F32)
    extras = [dummy if hn is None else hn.reshape(1, HEAD_DIM) for _, _, _, hn, _ in segs]
    in_specs = [pl.BlockSpec((TM, D), lambda i, j: (i, 0)), pl.BlockSpec((D, TN), lambda i, j: (0, j))]
    in_specs += [pl.BlockSpec((1, HEAD_DIM), lambda i, j: (0, 0))] * len(segs)
    if rope_tables is not None:
        extras += list(rope_tables)
        in_specs += [pl.BlockSpec((TM, HEAD_DIM), lambda i, j: (i % (seq // TM), 0))] * 2
    out_specs, out_shape, t0 = [], [], 0
    for (kind, width, dtype, _, _), nt in zip(segs, tiles):
        out_specs.append(pl.BlockSpec((TM, TN), lambda i, j, t0=t0, nt=nt: (i, jnp.clip(j - t0, 0, nt - 1))))
        out_shape.append(jax.ShapeDtypeStruct((N, nt * TN), dtype))
        t0 += nt
    vmem = 2 * (TM * D * 2 + D * TN * 2 + len(segs) * TM * TN * 4 + 2 * TM * LANES * 4) + 6 * TM * TN * 4
    return pl.pallas_call(
        functools.partial(_proj_kernel, segs=tuple((k, nt, sc) for (k, _, _, _, sc), nt in zip(segs, tiles)),
                          rope=rope_tables is not None),
        grid=(N // TM, sum(tiles)),
        in_specs=in_specs,
        out_specs=out_specs,
        out_shape=out_shape,
        compiler_params=_params(("parallel", "arbitrary"), vmem),
        name="proj",
    )(h2, w, *extras)


def _proj_weight(w_cols):
    pad = lambda w: jnp.pad(w.astype(BF16), ((0, 0), (0, -w.shape[1] % PROJ_TN)))
    return jnp.concatenate([pad(w) for w in w_cols], axis=1)


def _rope_tables(n_tok):
    rows = n_tok // GRID_W
    row = jnp.repeat(jnp.arange(rows, dtype=jnp.int32), GRID_W).astype(F32)
    col = jnp.tile(jnp.arange(GRID_W, dtype=jnp.int32), rows).astype(F32)
    axis_dim = HEAD_DIM // 2
    inv = ROPE_THETA ** (-jnp.arange(0, axis_dim, 2, dtype=F32) / axis_dim)
    ang_r = row[:, None] * inv[None]
    ang_c = col[:, None] * inv[None]
    cr, sr, cc, sc = jnp.cos(ang_r), jnp.sin(ang_r), jnp.cos(ang_c), jnp.sin(ang_c)
    return (jnp.concatenate([cr, cr, cc, cc], axis=-1), jnp.concatenate([-sr, sr, -sc, sc], axis=-1))


def _attn_kernel(q_ref, k_ref, v_ref, o_ref, *, TQ, TK):
    n_k = k_ref.shape[1] // TK
    q = jnp.concatenate([q_ref[0, :, h * HEAD_DIM:(h + 1) * HEAD_DIM] for h in range(Q_PER_KV)], axis=0)
    R = q.shape[0]
    scores = lambda j: _dotg(q, k_ref[0, j * TK:(j + 1) * TK, :], ((1,), (1,)))
    m = jnp.full((R, 1), -jnp.inf, F32)
    acc = jnp.zeros((R, 2 * HEAD_DIM), F32)
    s_next = scores(0)
    for j in range(n_k):
        s = s_next
        if j + 1 < n_k:
            s_next = scores(j + 1)
        m_new = jnp.maximum(m, jnp.max(s, axis=-1, keepdims=True))
        p = jnp.exp2(s - m_new).astype(BF16)
        acc = jnp.exp2(m - m_new) * acc + jnp.dot(p, v_ref[0, j * TK:(j + 1) * TK, :], preferred_element_type=F32)
        m = m_new
    o = acc[:, :HEAD_DIM] / acc[:, HEAD_DIM:]
    for h in range(Q_PER_KV):
        o_ref[0, :, h * HEAD_DIM:(h + 1) * HEAD_DIM] = o[h * TQ:(h + 1) * TQ, :].astype(o_ref.dtype)


def _attention(q, k, v):
    B, T, _ = q.shape
    Tk = k.shape[1]
    ones = jnp.ones((B, Tk, HEAD_DIM), v.dtype)
    v = jnp.concatenate([piece for g in range(N_KV_HEADS)
                         for piece in (v[:, :, g * HEAD_DIM:(g + 1) * HEAD_DIM], ones)], axis=-1)
    TQ = _tile(T, 256, 16)
    TK = _tile(Tk, 768, LANES)
    GW = Q_PER_KV * HEAD_DIM
    R = Q_PER_KV * TQ
    vmem = 2 * (2 * TQ * GW * 2 + 2 * Tk * HEAD_DIM * 2) + R * HEAD_DIM * (2 + 4 * 3) + 10 * R * TK * 4
    return pl.pallas_call(
        functools.partial(_attn_kernel, TQ=TQ, TK=TK),
        grid=(B, N_KV_HEADS, T // TQ),
        in_specs=[pl.BlockSpec((1, TQ, GW), lambda b, g, qi: (b, qi, g)),
                  pl.BlockSpec((1, Tk, HEAD_DIM), lambda b, g, qi: (b, 0, g)),
                  pl.BlockSpec((1, Tk, 2 * HEAD_DIM), lambda b, g, qi: (b, 0, g))],
        out_specs=pl.BlockSpec((1, TQ, GW), lambda b, g, qi: (b, qi, g)),
        out_shape=jax.ShapeDtypeStruct(q.shape, BF16),
        compiler_params=_params(("parallel", "parallel", "arbitrary"), vmem),
        name="gqa_flash",
    )(q, k, v)


def _segsum64(x, ones_bd):
    cols = []
    for c in range(x.shape[1] // LANES):
        cols.append(_mm(x[:, c * LANES:(c + 1) * LANES], ones_bd, na=2, nb=1))
    return jnp.concatenate(cols, axis=1)


def _prep_kernel(p_ref, pprev_ref, pnext_ref, mu_ref, w2_ref, w0_ref, a2_ref, a0_ref, g2_ref, kk_ref, ka_ref, rk_ref,
                 ones_ref, r_o, v_o, nkk_o, lw_o, km_o, bb_o, g_o, bv_o):
    p = p_ref[0]
    TT = p.shape[0]
    row = lax.broadcasted_iota(jnp.int32, (TT, 1), 0)
    i = pl.program_id(1)
    before = jnp.where(i == 0, 0.0, pprev_ref[0, SUBLANES - 1:SUBLANES, :])
    after = jnp.where(i == pl.num_programs(1) - 1, 0.0, pnext_ref[0, 0:1, :])
    prev = jnp.where(row == 0, before, pltpu.roll(p, 1, 0))
    nxt = jnp.where(row == TT - 1, after, pltpu.roll(p, TT - 1, 0))
    ps = p + mu_ref[...] * (0.5 * (prev + nxt) - p)
    W = RWKV_W
    r, k, v = ps[:, 0:W], ps[:, W:2 * W], ps[:, 2 * W:3 * W]
    lw = ps[:, 3 * W:3 * W + 2 * W_LORA]
    la = ps[:, 3 * W + 2 * W_LORA:3 * W + 2 * W_LORA + 2 * A_LORA]
    lg = ps[:, 3 * W + 2 * W_LORA + 2 * A_LORA:]
    ones_bd = ones_ref[...]
    terms = lambda ref: (ref[0], ref[1])
    w_raw = _mm(jnp.tanh(lw), terms(w2_ref), na=2) + w0_ref[...]
    logw = -DECAY_RATE * jax.nn.sigmoid(w_raw)
    a = jax.nn.sigmoid(_mm(la, terms(a2_ref), na=2) + a0_ref[...])
    g = _mm(jax.nn.sigmoid(lg), terms(g2_ref), na=2)
    kkv = k * kk_ref[...]
    kk = kkv * jnp.minimum(lax.rsqrt(_segsum64(kkv * kkv, ones_bd)), 1e12)
    ka = ka_ref[...]
    kmod_sum = None
    for z in range(2):
        az = a[:, z * W:(z + 1) * W]
        kmod = k * (1.0 + (az - 1.0) * ka)
        lw_o[z, 0] = logw[:, z * W:(z + 1) * W]
        km_o[z, 0] = kmod
        bb_o[z, 0] = kk * az
        kmod_sum = kmod if kmod_sum is None else kmod_sum + kmod
    bonus = _segsum64(r * kmod_sum * rk_ref[...], ones_bd)
    r_o[0] = r
    v_o[0] = v
    nkk_o[0] = -kk
    g_o[0] = g
    bv_o[0] = bonus * v


def _blockdiag2(m):
    _, R, C = m.shape
    z = jnp.zeros((R, C), m.dtype)
    return jnp.concatenate([jnp.concatenate([m[0], z], axis=1), jnp.concatenate([z, m[1]], axis=1)], axis=0)


def _rwkv_prep(p, mu, w0, w2, a0, a2, g2, k_k, k_a, r_k):
    B, T, _ = p.shape
    WI = RWKV_IN_W
    TT = _tile(T, 256, SUBLANES)
    nT = T // TT
    W = RWKV_W
    ones_bd = _blockdiag2(jnp.ones((2, RWKV_HEAD, RWKV_HEAD), F32))
    full = lambda a: pl.BlockSpec(a.shape, lambda b, i: (0,) * a.ndim)
    split2 = lambda w: jnp.stack(_split_bf16(w, 2))
    w2bd, a2bd, g2 = split2(_blockdiag2(w2)), split2(_blockdiag2(a2)), split2(g2)
    ins = [p, p, p, mu.reshape(1, WI), w2bd, w0.reshape(1, 2 * W), a2bd, a0.reshape(1, 2 * W), g2,
           k_k.reshape(1, W), k_a.reshape(1, W), r_k.reshape(1, W), ones_bd]
    rb = TT // SUBLANES
    in_specs = [pl.BlockSpec((1, TT, WI), lambda b, i: (b, i, 0)),
                pl.BlockSpec((1, SUBLANES, WI), lambda b, i: (b, jnp.maximum(i * rb - 1, 0), 0)),
                pl.BlockSpec((1, SUBLANES, WI), lambda b, i: (b, jnp.minimum((i + 1) * rb, nT * rb - 1), 0))]
    in_specs += [full(a) for a in ins[3:]]
    tok = jax.ShapeDtypeStruct((B, T, W), F32)
    tokz = jax.ShapeDtypeStruct((2, B, T, W), F32)
    s1 = pl.BlockSpec((1, TT, W), lambda b, i: (b, i, 0))
    s2 = pl.BlockSpec((2, 1, TT, W), lambda b, i: (0, b, i, 0))
    return pl.pallas_call(
        _prep_kernel,
        grid=(B, nT),
        in_specs=in_specs,
        out_specs=[s1, s1, s1, s2, s2, s2, s1, s1],
        out_shape=[tok, tok, tok, tokz, tokz, tokz, tok, tok],
        compiler_params=_params(("parallel", "parallel")),
        name="rwkv_prep",
    )(*ins)


def _sm(x, lo):
    return jnp.concatenate([jnp.where(lo, x, 0.0), jnp.where(lo, 0.0, x)], axis=0)


def _scan_chain(z, refs, hp, s_scr, y_ref, terms):
    r_ref, v_ref, a_ref, lw_ref, km_ref, bb_ref = refs
    C = CHUNK
    sl = slice(hp * LANES, (hp + 1) * LANES)
    sgn = 1 - 2 * z
    ri = lax.broadcasted_iota(jnp.int32, (2 * C, 2 * C), 0)
    ci = lax.broadcasted_iota(jnp.int32, (2 * C, 2 * C), 1)
    strict = (ri - ci) * sgn > 0
    incl = (ri - ci) * sgn >= 0
    eye = jnp.where(ri == ci, 1.0, 0.0)
    lo = lax.broadcasted_iota(jnp.int32, (C, LANES), 1) < RWKV_HEAD
    row = lax.broadcasted_iota(jnp.int32, (C, LANES), 0)

    lw = lw_ref[0, 0, :, sl]
    cum = lw
    for k in range(int(math.log2(C))):
        d = 1 << k
        if z == 0:
            cum = cum + jnp.where(row >= d, pltpu.roll(cum, d, 0), 0.0)
        else:
            cum = cum + jnp.where(row < C - d, pltpu.roll(cum, C - d, 0), 0.0)
    tot = jnp.sum(lw, axis=0, keepdims=True)
    dec_in = jnp.exp(cum - lw)
    dec_out = jnp.exp(cum)
    inv = jnp.exp(-cum)
    rest = jnp.exp(tot - cum)
    a_t = _sm(a_ref[0, :, sl] * dec_in, lo)
    r_t = _sm(r_ref[0, :, sl] * dec_out, lo)
    bb = bb_ref[0, 0, :, sl]
    km = km_ref[0, 0, :, sl]
    b_t = _sm(bb * inv, lo)
    k_t = _sm(km * inv, lo)
    bk_end = jnp.concatenate([_sm(bb * rest, lo), _sm(km * rest, lo)], axis=0)
    v_s = _sm(v_ref[0, :, sl], lo)
    ar = jnp.concatenate([a_t, r_t], axis=0)
    G = _mm(ar, jnp.concatenate([b_t, k_t], axis=0), dims=((1,), (1,)), na=terms[0], nb=terms[0])
    S = s_scr[z, hp]
    ms = _mm(ar, S, dims=((1,), (1,)), na=terms[2], nb=terms[2])
    yield
    L = jnp.where(strict, G[:2 * C, :2 * C], 0.0)
    Lak = jnp.where(strict, G[:2 * C, 2 * C:], 0.0)
    Grb = jnp.where(incl, G[2 * C:, :2 * C], 0.0)
    Grk = jnp.where(incl, G[2 * C:, 2 * C:], 0.0)
    X = ms[:2 * C] + _mm(Lak, v_s, na=terms[3], nb=terms[3])

    T = eye + jnp.where((ri >> 1) == (ci >> 1), L, 0.0)
    for lvl in range(1, int(math.log2(C))):
        off = ((ri >> (lvl + 1)) == (ci >> (lvl + 1))) & ((ri >> lvl) != (ci >> lvl))
        TL = _mm(T, jnp.where(off, L, 0.0), na=terms[1], nb=terms[1])
        yield
        T = T + _mm(TL, T, na=terms[1], nb=terms[1])
        yield

    U = _mm(T, X, na=terms[4][0], nb=terms[4][1])
    yield
    uv = jnp.concatenate([U, v_s], axis=0)
    Y = ms[2 * C:] + _mm(jnp.concatenate([Grb, Grk], axis=1), uv, na=terms[5], nb=terms[5])
    y_ref[0, :, sl] = Y[:C] + Y[C:]
    s_scr[z, hp] = S * jnp.exp(tot) + _mm(uv, bk_end, dims=((0,), (0,)), na=terms[6], nb=terms[6])


def _scan_kernel(rf, vf, af, lwf, kmf, bbf, rb, vb, ab, lwb, kmb, bbb, s0_ref, yf_ref, yb_ref, sN_ref, s_scr, *, terms):
    c = pl.program_id(1)

    @pl.when(c == 0)
    def _():
        s_scr[...] = s0_ref[:, 0]

    n_pairs = s_scr.shape[1]
    chains = [_scan_chain(0, (rf, vf, af, lwf, kmf, bbf), hp, s_scr, yf_ref, terms) for hp in range(n_pairs)]
    chains += [_scan_chain(1, (rb, vb, ab, lwb, kmb, bbb), hp, s_scr, yb_ref, terms) for hp in range(n_pairs)]
    for _ in zip(*chains):
        pass
    for ch in chains:
        for _ in ch:
            pass

    @pl.when(c == pl.num_programs(1) - 1)
    def _():
        sN_ref[:, 0] = s_scr[...]


SCAN_TERMS = (1, 1, 1, 1, (1, 2), 1, 1)


def _rwkv_scan(r, v, nkk, lw, km, bb, s0, terms=SCAN_TERMS):
    B, T, W = r.shape
    C = CHUNK
    NC = T // C
    NP = W // LANES
    fwd = lambda b, c: (b, c, 0)
    bwd = lambda b, c: (b, NC - 1 - c, 0)
    sh_f, sh_b = pl.BlockSpec((1, C, W), fwd), pl.BlockSpec((1, C, W), bwd)
    pd_f = pl.BlockSpec((1, 1, C, W), lambda b, c: (0, b, c, 0))
    pd_b = pl.BlockSpec((1, 1, C, W), lambda b, c: (1, b, NC - 1 - c, 0))
    state = pl.BlockSpec((2, 1, NP, LANES, LANES), lambda b, c: (0, b, 0, 0, 0))
    tok = jax.ShapeDtypeStruct((B, T, W), F32)
    return pl.pallas_call(
        functools.partial(_scan_kernel, terms=terms),
        grid=(B, NC),
        in_specs=[sh_f, sh_f, sh_f, pd_f, pd_f, pd_f, sh_b, sh_b, sh_b, pd_b, pd_b, pd_b, state],
        out_specs=[sh_f, sh_b, state],
        out_shape=[tok, tok, jax.ShapeDtypeStruct(s0.shape, F32)],
        scratch_shapes=[pltpu.VMEM((2, NP, LANES, LANES), F32)],
        compiler_params=_params(("parallel", "arbitrary")),
        name="rwkv_scan",
    )(r, v, nkk, lw, km, bb, r, v, nkk, lw, km, bb, s0)


def _rwkv_out_kernel(yf_ref, yb_ref, bv_ref, g_ref, lnw_ref, lnb_ref, ones_ref, o_ref):
    y = yf_ref[0] + yb_ref[0]
    ones_bd = ones_ref[...]
    inv_n = 1.0 / RWKV_HEAD
    mu = _segsum64(y, ones_bd) * inv_n
    d = y - mu
    var = _segsum64(d * d, ones_bd) * inv_n
    yn = d * lax.rsqrt(var + GN_EPS) * lnw_ref[...] + lnb_ref[...]
    o_ref[0] = ((yn + bv_ref[0]) * g_ref[0]).astype(o_ref.dtype)


def _rwkv_out(yf, yb, bv, g, ln_w, ln_b):
    B, T, W = yf.shape
    TT = _tile(T, 512, SUBLANES)
    ones_bd = _blockdiag2(jnp.ones((2, RWKV_HEAD, RWKV_HEAD), F32))
    s1 = pl.BlockSpec((1, TT, W), lambda b, i: (b, i, 0))
    return pl.pallas_call(
        _rwkv_out_kernel,
        grid=(B, T // TT),
        in_specs=[s1, s1, s1, s1,
                  pl.BlockSpec((1, W), lambda b, i: (0, 0)), pl.BlockSpec((1, W), lambda b, i: (0, 0)),
                  pl.BlockSpec((LANES, LANES), lambda b, i: (0, 0))],
        out_specs=s1,
        out_shape=jax.ShapeDtypeStruct((B, T, W), BF16),
        compiler_params=_params(("parallel", "parallel")),
        name="rwkv_out",
    )(yf, yb, bv, g, ln_w.reshape(1, W), ln_b.reshape(1, W), ones_bd)


def _mixout_kernel(ao_ref, ro_ref, woa_ref, wor_ref, ga_ref, gr_ref, wo_ref, x_ref, mod_ref, o_ref, t_scr, *, mod_base):
    nj, _, TN = t_scr.shape
    j = pl.program_id(1)

    @pl.when(j < nj)
    def _():
        ta = jnp.dot(ao_ref[...], woa_ref[...], preferred_element_type=F32)
        tr = jnp.dot(ro_ref[...], wor_ref[...], preferred_element_type=F32)
        t_scr[j] = (ga_ref[...].astype(F32) * ta + gr_ref[...].astype(F32) * tr).astype(BF16)

    @pl.when(j >= nj)
    def _():
        acc = None
        for k in range(nj):
            part = jnp.dot(t_scr[k], wo_ref[k * TN:(k + 1) * TN, :], preferred_element_type=F32)
            acc = part if acc is None else acc + part
        o_ref[...] = x_ref[...] + mod_ref[0, mod_base:mod_base + 1, :] * acc


def _mixout(attn_o, rwkv_o, w_oa, w_or, gates, w_out, x2, modtab, mod_row, mod_base):
    N, DA = attn_o.shape
    DR = rwkv_o.shape[1]
    D = w_oa.shape[1]
    seq, mod_row = mod_row
    TM = _tile(seq, 1024, 16)
    TN = _tile(D, 512, LANES)
    nj = D // TN
    first = lambda j: jnp.minimum(j, nj - 1)
    second = lambda j: jnp.maximum(j - nj, 0)
    vmem = (2 * (TM * (DA + DR) * 2 + (DA + DR + D) * TN * 2 + 2 * TM * TN * 2 + 2 * TM * TN * 4)
            + TM * D * 2 + 6 * TM * TN * 4)
    return pl.pallas_call(
        functools.partial(_mixout_kernel, mod_base=mod_base),
        grid=(N // TM, 2 * nj),
        in_specs=[pl.BlockSpec((TM, DA), lambda i, j: (i, 0)),
                  pl.BlockSpec((TM, DR), lambda i, j: (i, 0)),
                  pl.BlockSpec((DA, TN), lambda i, j: (0, first(j))),
                  pl.BlockSpec((DR, TN), lambda i, j: (0, first(j))),
                  pl.BlockSpec((TM, TN), lambda i, j: (i, first(j))),
                  pl.BlockSpec((TM, TN), lambda i, j: (i, nj + first(j))),
                  pl.BlockSpec((D, TN), lambda i, j: (0, second(j))),
                  pl.BlockSpec((TM, TN), lambda i, j: (i, second(j))),
                  pl.BlockSpec((1, N_MOD, TN), lambda i, j: (mod_row(i, TM), 0, second(j)))],
        out_specs=pl.BlockSpec((TM, TN), lambda i, j: (i, second(j))),
        out_shape=jax.ShapeDtypeStruct((N, D), F32),
        scratch_shapes=[pltpu.VMEM((nj, TM, TN), BF16)],
        compiler_params=_params(("parallel", "arbitrary"), vmem),
        name="mixout",
    )(attn_o, rwkv_o, w_oa, w_or, gates, gates, w_out, x2, modtab)


def kernel(x, c, ctx, c_ctx, w_mod, b_mod, norm_ffn1, ffn1_w_in, ffn1_w_out, norm_mix, w_in, q_norm, k_norm, rwkv_mu, w0, w2, a0, a2, g2, k_k, k_a, r_k, ln_x_w, ln_x_b, w_oa, w_or, w_out, norm_ffn2, ffn2_w_in, ffn2_w_out, norm_final):
    B, S, D = x.shape
    CT = ctx.shape[1]
    assert w_mod.shape[0] == 1, "single layer"
    ATT_Q, ATT_KV = N_Q_HEADS * HEAD_DIM, N_KV_HEADS * HEAD_DIM
    assert w_in.shape[2] == ATT_Q + 2 * ATT_KV + RWKV_IN_W + 2 * D
    assert S % CHUNK == 0 and CT % CHUNK == 0 and S % GRID_W == 0

    lat_row = (S, lambda i, tm: (i * tm) // S)
    ctx_row = (B * CT, lambda i, tm: B)
    bf = lambda w: w.astype(BF16)

    modtab = _mod_table(c, c_ctx, w_mod[0], b_mod[0])

    x2 = x.reshape(B * S, D)
    cx2 = ctx.reshape(B * CT, D)
    w1_in, w1_out = bf(ffn1_w_in[0]), bf(ffn1_w_out[0])
    x2, h2 = _ffn(x2, modtab, lat_row, 0, norm_ffn1[0], w1_in, w1_out, next_norm=(norm_mix[0], 3))
    _, hc2 = _ffn(cx2, modtab, ctx_row, 0, norm_ffn1[0], w1_in, w1_out, next_norm=(norm_mix[0], 3))

    wi = w_in[0]
    o = 0
    wq = wi[:, o:o + ATT_Q]; o += ATT_Q
    wk = wi[:, o:o + ATT_KV]; o += ATT_KV
    wv = wi[:, o:o + ATT_KV]; o += ATT_KV
    wr = wi[:, o:o + RWKV_IN_W]; o += RWKV_IN_W
    wg = wi[:, o:]
    q_seg = ("headnorm_rope", ATT_Q, BF16, q_norm[0], HEAD_DIM ** -0.5 * LOG2E)
    v_seg = ("plain", ATT_KV, BF16, None, 1.0)
    r_seg = ("plain", RWKV_IN_W, F32, None, 1.0)
    g_seg = ("sigmoid", 2 * D, BF16, None, 1.0)
    q, k, v, p_rw, gates = _proj(h2, _proj_weight([wq, wk, wv, wr, wg]), S,
                                 [q_seg, ("headnorm_rope", ATT_KV, BF16, k_norm[0], 1.0), v_seg, r_seg, g_seg],
                                 rope_tables=_rope_tables(S))
    kc, vc, p_rw_c = _proj(hc2, _proj_weight([wk, wv, wr]), B * CT,
                           [("headnorm", ATT_KV, BF16, k_norm[0], 1.0), v_seg, r_seg])

    k_all = jnp.concatenate([k.reshape(B, S, ATT_KV), kc.reshape(B, CT, ATT_KV)], axis=1)
    v_all = jnp.concatenate([v.reshape(B, S, ATT_KV), vc.reshape(B, CT, ATT_KV)], axis=1)
    attn_o = _attention(q.reshape(B, S, ATT_Q), k_all, v_all).reshape(B * S, ATT_Q)

    rw = (rwkv_mu[0], w0[0], w2[0], a0[0], a2[0], g2[0], k_k[0], k_a[0], r_k[0])
    rc, vcr, akc, lwc, kmc, bbc, _, _ = _rwkv_prep(p_rw_c.reshape(B, CT, -1), *rw)
    s0 = jnp.zeros((2, B, RWKV_W // LANES, LANES, LANES), F32)
    _, _, s_ctx = _rwkv_scan(rc, vcr, akc, lwc, kmc, bbc, s0)
    rl, vl, akl, lwl, kml, bbl, gl, bvl = _rwkv_prep(p_rw.reshape(B, S, -1), *rw)
    yf, yb, _ = _rwkv_scan(rl, vl, akl, lwl, kml, bbl, s_ctx)
    rwkv_o = _rwkv_out(yf, yb, bvl, gl, ln_x_w[0], ln_x_b[0]).reshape(B * S, RWKV_W)

    x2 = _mixout(attn_o, rwkv_o, bf(w_oa[0]), bf(w_or[0]), gates, bf(w_out[0]), x2, modtab, lat_row, 5)

    out = _ffn(x2, modtab, lat_row, 6, norm_ffn2[0], bf(ffn2_w_in[0]), bf(ffn2_w_out[0]), g_final=norm_final)
    return out.reshape(B, S, D)
```

```python
import functools
import math

import jax
import jax.numpy as jnp
from jax import lax
from jax.experimental import pallas as pl
from jax.experimental.pallas import tpu as pltpu

F32 = jnp.float32
BF16 = jnp.bfloat16

N_Q_HEADS = 16
N_KV_HEADS = 4
HEAD_DIM = 128
Q_PER_KV = N_Q_HEADS // N_KV_HEADS
RWKV_HEADS = 16
RWKV_HEAD = 64
RWKV_W = RWKV_HEADS * RWKV_HEAD
W_LORA = 64
A_LORA = 64
G_LORA = 128
RWKV_IN_W = 3 * RWKV_W + 2 * W_LORA + 2 * A_LORA + G_LORA
GRID_W = 64
ROPE_THETA = 10000.0
N_MOD = 9
EPS = 1e-6
GN_EPS = 64e-5
LOG2E = math.log2(math.e)
DECAY_RATE = math.exp(-0.5)

LANES = 128
SUBLANES = 8
VMEM_BYTES_V7X = 64 * 1024 * 1024
VMEM_LIMIT = 60000 * 1024

CHUNK = 64
PAIR = 2 * RWKV_HEAD
assert PAIR == LANES


def _tile(n, pref, mult):
    if n <= pref:
        return n
    best = None
    for t in range(mult, pref + 1, mult):
        if n % t == 0:
            best = t
    assert best is not None, (n, pref, mult)
    return best


def _params(sem, vmem=None):
    return pltpu.CompilerParams(dimension_semantics=sem, vmem_limit_bytes=min(vmem or VMEM_LIMIT, VMEM_LIMIT))


def _split_bf16(x, n):
    terms, rem = [], x
    for _ in range(n):
        t = rem.astype(BF16)
        terms.append(t)
        rem = rem - t.astype(F32)
    return terms


def _dotg(a, b, dims):
    return lax.dot_general(a, b, (dims, ((), ())), preferred_element_type=F32)


def _mm(a, b, dims=((1,), (0,)), na=1, nb=1):
    at = _split_bf16(a, na) if a.dtype != BF16 else [a]
    if isinstance(b, (list, tuple)):
        bt = list(b)
    else:
        bt = _split_bf16(b, nb) if b.dtype != BF16 else [b]
    order = max(len(at), len(bt))
    acc = None
    for i, x in enumerate(at):
        for j, y in enumerate(bt):
            if i + j < order:
                d = _dotg(x, y, dims)
                acc = d if acc is None else acc + d
    return acc


def _rmsnorm(x, g):
    ms = jnp.mean(x * x, axis=-1, keepdims=True)
    return x * lax.rsqrt(ms + EPS) * g


def _mod_kernel(c_ref, w_ref, b_ref, o_ref):
    c = c_ref[...]
    s = (c * jax.nn.sigmoid(c)).astype(BF16)
    o_ref[...] = jnp.dot(s, w_ref[...].astype(BF16), preferred_element_type=F32) + b_ref[...]


def _mod_table(c, c_ctx, w_mod, b_mod):
    B, D = c.shape
    NO = w_mod.shape[1]
    rows = -(-(B + 1) // SUBLANES) * SUBLANES
    cin = jnp.zeros((rows, D), F32).at[:B].set(c).at[B].set(c_ctx)
    TN = _tile(NO, 1024, LANES)
    out = pl.pallas_call(
        _mod_kernel,
        grid=(NO // TN,),
        in_specs=[pl.BlockSpec((rows, D), lambda j: (0, 0)),
                  pl.BlockSpec((D, TN), lambda j: (0, j)),
                  pl.BlockSpec((1, TN), lambda j: (0, j))],
        out_specs=pl.BlockSpec((rows, TN), lambda j: (0, j)),
        out_shape=jax.ShapeDtypeStruct((rows, NO), F32),
        compiler_params=_params(("parallel",)),
        name="mod_table",
    )(cin, w_mod, b_mod.reshape(1, NO))
    return out[:B + 1].reshape(B + 1, N_MOD, D)


def _ffn_kernel(x_ref, mod_ref, g_ref, wg_ref, wu_ref, wo_ref, *rest, mod_base, final_norm, next_base):
    gf_ref = gn_ref = hn_ref = None
    if final_norm:
        gf_ref, o_ref, h_scr, acc_scr = rest
    elif next_base is not None:
        gn_ref, o_ref, hn_ref, h_scr, acc_scr = rest
    else:
        o_ref, h_scr, acc_scr = rest
    f = pl.program_id(1)
    last = pl.num_programs(1) - 1
    TM = x_ref.shape[0]
    halves = [pl.ds(0, TM // 2), pl.ds(TM // 2, TM // 2)] if TM % 32 == 0 else [pl.ds(0, TM)]

    def prologue(rows):
        shift = mod_ref[0, mod_base:mod_base + 1, :]
        scale = mod_ref[0, mod_base + 1:mod_base + 2, :]
        h = _rmsnorm(x_ref[rows, :], g_ref[...]) * (1.0 + scale) + shift
        h_scr[rows, :] = h.astype(BF16)

    def chunk(rows, first):
        h = h_scr[rows, :]
        gt = jnp.dot(h, wg_ref[...], preferred_element_type=F32)
        up = jnp.dot(h, wu_ref[...], preferred_element_type=F32)
        act = (gt * jax.nn.sigmoid(gt) * up).astype(BF16)
        part = jnp.dot(act, wo_ref[...], preferred_element_type=F32)
        acc_scr[rows, :] = part if first else acc_scr[rows, :] + part

    def epilogue(rows):
        gate = mod_ref[0, mod_base + 2:mod_base + 3, :]
        out = x_ref[rows, :] + 0.5 * gate * acc_scr[rows, :]
        if final_norm:
            out = _rmsnorm(out, gf_ref[...])
        o_ref[rows, :] = out
        if next_base is not None:
            shift = mod_ref[0, next_base:next_base + 1, :]
            scale = mod_ref[0, next_base + 1:next_base + 2, :]
            hn_ref[rows, :] = (_rmsnorm(out, gn_ref[...]) * (1.0 + scale) + shift).astype(BF16)

    @pl.when(f == 0)
    def _():
        for rows in halves:
            prologue(rows)
            chunk(rows, first=True)

    @pl.when((f > 0) & (f < last))
    def _():
        chunk(pl.ds(0, TM), first=False)

    @pl.when(f == last)
    def _():
        for rows in halves:
            chunk(rows, first=False)
            epilogue(rows)


def _ffn(x2, modtab, mod_row, mod_base, g, w_in, w_out, g_final=None, next_norm=None):
    N, D = x2.shape
    F = w_out.shape[0]
    seq, mod_row = mod_row
    TM = _tile(seq, 512, SUBLANES)
    TF = _tile(F, 512, LANES)
    nf = F // TF
    assert nf >= 2, "first and last d_ff chunk are distinct grid steps"
    in_specs = [pl.BlockSpec((TM, D), lambda i, f: (i, 0)),
                pl.BlockSpec((1, N_MOD, D), lambda i, f: (mod_row(i, TM), 0, 0)),
                pl.BlockSpec((1, D), lambda i, f: (0, 0)),
                pl.BlockSpec((D, TF), lambda i, f: (0, f)),
                pl.BlockSpec((D, TF), lambda i, f: (0, nf + f)),
                pl.BlockSpec((TF, D), lambda i, f: (f, 0))]
    args = [x2, modtab, g.reshape(1, D), w_in, w_in, w_out]
    assert g_final is None or next_norm is None
    row_spec = pl.BlockSpec((TM, D), lambda i, f: (i, 0))
    out_specs, out_shape = row_spec, jax.ShapeDtypeStruct((N, D), F32)
    if g_final is not None or next_norm is not None:
        in_specs.append(pl.BlockSpec((1, D), lambda i, f: (0, 0)))
        args.append((g_final if g_final is not None else next_norm[0]).reshape(1, D))
    if next_norm is not None:
        out_specs, out_shape = [row_spec, row_spec], [out_shape, jax.ShapeDtypeStruct((N, D), BF16)]
    vmem = 2 * (2 * TM * D * 4 + TM * D * 2 + 3 * D * TF * 2) + TM * D * 6 + 4 * TM * TF * 4
    return pl.pallas_call(
        functools.partial(_ffn_kernel, mod_base=mod_base, final_norm=g_final is not None,
                          next_base=None if next_norm is None else next_norm[1]),
        grid=(N // TM, nf),
        in_specs=in_specs,
        out_specs=out_specs,
        out_shape=out_shape,
        scratch_shapes=[pltpu.VMEM((TM, D), BF16), pltpu.VMEM((TM, D), F32)],
        compiler_params=_params(("parallel", "arbitrary"), vmem + (8 << 20)),
        name="ffn",
    )(*args)


def _swap32(t):
    lane = lax.broadcasted_iota(jnp.int32, t.shape, t.ndim - 1)
    return jnp.where((lane % 64) >= 32, pltpu.roll(t, 32, t.ndim - 1), pltpu.roll(t, LANES - 32, t.ndim - 1))


PROJ_TN = 512


def _proj_kernel(h_ref, w_ref, *rest, segs, rope):
    n_extra = len(segs) + (2 if rope else 0)
    extras, outs = rest[:n_extra], rest[n_extra:]
    j = pl.program_id(1)
    TM = h_ref.shape[0]
    n_split = 2 if TM % 32 == 0 else 1
    t0 = 0
    for si, (kind, n_tiles, scale) in enumerate(segs):
        def tile(kind=kind, scale=scale, o_ref=outs[si], hn_ref=extras[si]):
            for rows in (pl.ds(r * (TM // n_split), TM // n_split) for r in range(n_split)):
                acc = jnp.dot(h_ref[rows, :], w_ref[...], preferred_element_type=F32)
                if kind == "plain":
                    o_ref[rows, :] = acc.astype(o_ref.dtype)
                elif kind == "head_ones":
                    for hd in range(acc.shape[1] // HEAD_DIM):
                        o_ref[rows, 2 * hd * HEAD_DIM:(2 * hd + 1) * HEAD_DIM] = (
                            acc[:, hd * HEAD_DIM:(hd + 1) * HEAD_DIM].astype(o_ref.dtype))
                        o_ref[rows, (2 * hd + 1) * HEAD_DIM:(2 * hd + 2) * HEAD_DIM] = (
                            jnp.ones((acc.shape[0], HEAD_DIM), o_ref.dtype))
                elif kind == "sigmoid":
                    o_ref[rows, :] = (0.5 * jnp.tanh(0.5 * acc) + 0.5).astype(o_ref.dtype)
                else:
                    ones = jnp.ones((HEAD_DIM, HEAD_DIM), BF16)
                    for hd in range(acc.shape[1] // HEAD_DIM):
                        t = acc[:, hd * HEAD_DIM:(hd + 1) * HEAD_DIM]
                        ms = _mm(t * t, ones, na=2, nb=1) * (1.0 / HEAD_DIM)
                        t = t * lax.rsqrt(ms + EPS) * hn_ref[...]
                        if kind == "headnorm_rope":
                            t = t * extras[-2][rows, :] + _swap32(t) * extras[-1][rows, :]
                        o_ref[rows, hd * HEAD_DIM:(hd + 1) * HEAD_DIM] = (t * scale).astype(o_ref.dtype)
        pl.when((j >= t0) & (j < t0 + n_tiles))(tile)
        t0 += n_tiles


def _proj(h2, w, seq, segs, rope_tables=None):
    N, D = h2.shape
    TN = PROJ_TN
    TM = _tile(seq, 1024, 16)
    tiles = [-(-width // TN) for _, width, _, _, _ in segs]
    assert w.shape[1] == TN * sum(tiles)
    dummy = jnp.zeros((1, HEAD_DIM), F32)
    extras = [dummy if hn is None else hn.reshape(1, HEAD_DIM) for _, _, _, hn, _ in segs]
    in_specs = [pl.BlockSpec((TM, D), lambda i, j: (i, 0)), pl.BlockSpec((D, TN), lambda i, j: (0, j))]
    in_specs += [pl.BlockSpec((1, HEAD_DIM), lambda i, j: (0, 0))] * len(segs)
    if rope_tables is not None:
        extras += list(rope_tables)
        in_specs += [pl.BlockSpec((TM, HEAD_DIM), lambda i, j: (i % (seq // TM), 0))] * 2
    out_specs, out_shape, t0 = [], [], 0
    for (kind, width, dtype, _, _), nt in zip(segs, tiles):
        ow = 2 * TN if kind == "head_ones" else TN
        out_specs.append(pl.BlockSpec((TM, ow), lambda i, j, t0=t0, nt=nt: (i, jnp.clip(j - t0, 0, nt - 1))))
        out_shape.append(jax.ShapeDtypeStruct((N, nt * ow), dtype))
        t0 += nt
    vmem = 2 * (TM * D * 2 + D * TN * 2 + len(segs) * TM * TN * 4 + 2 * TM * LANES * 4) + 6 * TM * TN * 4
    return pl.pallas_call(
        functools.partial(_proj_kernel, segs=tuple((k, nt, sc) for (k, _, _, _, sc), nt in zip(segs, tiles)),
                          rope=rope_tables is not None),
        grid=(N // TM, sum(tiles)),
        in_specs=in_specs,
        out_specs=out_specs,
        out_shape=out_shape,
        compiler_params=_params(("parallel", "arbitrary"), vmem),
        name="proj",
    )(h2, w, *extras)


def _proj_weight(w_cols):
    pad = lambda w: jnp.pad(w.astype(BF16), ((0, 0), (0, -w.shape[1] % PROJ_TN)))
    return jnp.concatenate([pad(w) for w in w_cols], axis=1)


def _rope_tables(n_tok):
    rows = n_tok // GRID_W
    row = jnp.repeat(jnp.arange(rows, dtype=jnp.int32), GRID_W).astype(F32)
    col = jnp.tile(jnp.arange(GRID_W, dtype=jnp.int32), rows).astype(F32)
    axis_dim = HEAD_DIM // 2
    inv = ROPE_THETA ** (-jnp.arange(0, axis_dim, 2, dtype=F32) / axis_dim)
    ang_r = row[:, None] * inv[None]
    ang_c = col[:, None] * inv[None]
    cr, sr, cc, sc = jnp.cos(ang_r), jnp.sin(ang_r), jnp.cos(ang_c), jnp.sin(ang_c)
    return (jnp.concatenate([cr, cr, cc, cc], axis=-1), jnp.concatenate([-sr, sr, -sc, sc], axis=-1))


ATTN_TK = 768


def _attn_kernel(q_ref, k_ref, v_ref, kc_ref, vc_ref, o_ref, *, TQ):
    S = k_ref.shape[1]
    n_full = (S - 1) // ATTN_TK
    tail = n_full * ATTN_TK

    def kv(j):
        if j < n_full:
            rows = slice(j * ATTN_TK, (j + 1) * ATTN_TK)
            return k_ref[0, rows, :], v_ref[0, rows, :]
        return (jnp.concatenate([k_ref[0, tail:, :], kc_ref[0]], axis=0),
                jnp.concatenate([v_ref[0, tail:, :], vc_ref[0]], axis=0))

    q = jnp.concatenate([q_ref[0, :, h * HEAD_DIM:(h + 1) * HEAD_DIM] for h in range(Q_PER_KV)], axis=0)
    R = q.shape[0]
    tiles = [kv(j) for j in range(n_full + 1)]
    scores = lambda j: _dotg(q, tiles[j][0], ((1,), (1,)))
    m = jnp.full((R, 1), -jnp.inf, F32)
    acc = jnp.zeros((R, 2 * HEAD_DIM), F32)
    s_next = scores(0)
    for j in range(n_full + 1):
        s = s_next
        if j < n_full:
            s_next = scores(j + 1)
        m_new = jnp.maximum(m, jnp.max(s, axis=-1, keepdims=True))
        p = jnp.exp2(s - m_new).astype(BF16)
        acc = jnp.exp2(m - m_new) * acc + jnp.dot(p, tiles[j][1], preferred_element_type=F32)
        m = m_new
    o = acc[:, :HEAD_DIM] / acc[:, HEAD_DIM:]
    for h in range(Q_PER_KV):
        o_ref[0, :, h * HEAD_DIM:(h + 1) * HEAD_DIM] = o[h * TQ:(h + 1) * TQ, :].astype(o_ref.dtype)


def _attention(q, k, v, kc, vc):
    B, S, _ = q.shape
    CT = kc.shape[1]
    TQ = _tile(S, 256, 16)
    GW = Q_PER_KV * HEAD_DIM
    R = Q_PER_KV * TQ
    vmem = 2 * (2 * TQ * GW * 2 + 3 * (S + CT) * HEAD_DIM * 2) + R * HEAD_DIM * (2 + 4 * 3) + 10 * R * ATTN_TK * 4
    kv_spec = lambda rows, width: pl.BlockSpec((1, rows, width), lambda b, g, qi: (b, 0, g))
    return pl.pallas_call(
        functools.partial(_attn_kernel, TQ=TQ),
        grid=(B, N_KV_HEADS, S // TQ),
        in_specs=[pl.BlockSpec((1, TQ, GW), lambda b, g, qi: (b, qi, g)),
                  kv_spec(S, HEAD_DIM), kv_spec(S, 2 * HEAD_DIM), kv_spec(CT, HEAD_DIM), kv_spec(CT, 2 * HEAD_DIM)],
        out_specs=pl.BlockSpec((1, TQ, GW), lambda b, g, qi: (b, qi, g)),
        out_shape=jax.ShapeDtypeStruct(q.shape, BF16),
        compiler_params=_params(("parallel", "parallel", "arbitrary"), vmem),
        name="gqa_flash",
    )(q, k, v, kc, vc)


def _segsum64(x, ones_bd):
    cols = []
    for c in range(x.shape[1] // LANES):
        cols.append(_mm(x[:, c * LANES:(c + 1) * LANES], ones_bd, na=2, nb=1))
    return jnp.concatenate(cols, axis=1)


def _prep_kernel(p_ref, pprev_ref, pnext_ref, mu_ref, w2_ref, w0_ref, a2_ref, a0_ref, g2_ref, kk_ref, ka_ref, rk_ref,
                 ones_ref, r_o, v_o, nkk_o, lw_o, km_o, bb_o, g_o, bv_o):
    p = p_ref[0]
    TT = p.shape[0]
    row = lax.broadcasted_iota(jnp.int32, (TT, 1), 0)
    i = pl.program_id(1)
    before = jnp.where(i == 0, 0.0, pprev_ref[0, SUBLANES - 1:SUBLANES, :])
    after = jnp.where(i == pl.num_programs(1) - 1, 0.0, pnext_ref[0, 0:1, :])
    prev = jnp.where(row == 0, before, pltpu.roll(p, 1, 0))
    nxt = jnp.where(row == TT - 1, after, pltpu.roll(p, TT - 1, 0))
    ps = p + mu_ref[...] * (0.5 * (prev + nxt) - p)
    W = RWKV_W
    r, k, v = ps[:, 0:W], ps[:, W:2 * W], ps[:, 2 * W:3 * W]
    lw = ps[:, 3 * W:3 * W + 2 * W_LORA]
    la = ps[:, 3 * W + 2 * W_LORA:3 * W + 2 * W_LORA + 2 * A_LORA]
    lg = ps[:, 3 * W + 2 * W_LORA + 2 * A_LORA:]
    ones_bd = ones_ref[...]
    terms = lambda ref: (ref[0], ref[1])
    w_raw = _mm(jnp.tanh(lw), terms(w2_ref), na=2) + w0_ref[...]
    logw = -DECAY_RATE * jax.nn.sigmoid(w_raw)
    a = jax.nn.sigmoid(_mm(la, terms(a2_ref), na=2) + a0_ref[...])
    g = _mm(jax.nn.sigmoid(lg), terms(g2_ref), na=2)
    kkv = k * kk_ref[...]
    kk = kkv * jnp.minimum(lax.rsqrt(_segsum64(kkv * kkv, ones_bd)), 1e12)
    ka = ka_ref[...]
    kmod_sum = None
    for z in range(2):
        az = a[:, z * W:(z + 1) * W]
        kmod = k * (1.0 + (az - 1.0) * ka)
        lw_o[z, 0] = logw[:, z * W:(z + 1) * W]
        km_o[z, 0] = kmod
        bb_o[z, 0] = kk * az
        kmod_sum = kmod if kmod_sum is None else kmod_sum + kmod
    bonus = _segsum64(r * kmod_sum * rk_ref[...], ones_bd)
    r_o[0] = r
    v_o[0] = v
    nkk_o[0] = -kk
    g_o[0] = g
    bv_o[0] = bonus * v


def _blockdiag2(m):
    _, R, C = m.shape
    z = jnp.zeros((R, C), m.dtype)
    return jnp.concatenate([jnp.concatenate([m[0], z], axis=1), jnp.concatenate([z, m[1]], axis=1)], axis=0)


def _rwkv_prep(p, mu, w0, w2, a0, a2, g2, k_k, k_a, r_k):
    B, T, _ = p.shape
    WI = RWKV_IN_W
    TT = _tile(T, 256, SUBLANES)
    nT = T // TT
    W = RWKV_W
    ones_bd = _blockdiag2(jnp.ones((2, RWKV_HEAD, RWKV_HEAD), F32))
    full = lambda a: pl.BlockSpec(a.shape, lambda b, i: (0,) * a.ndim)
    split2 = lambda w: jnp.stack(_split_bf16(w, 2))
    w2bd, a2bd, g2 = split2(_blockdiag2(w2)), split2(_blockdiag2(a2)), split2(g2)
    ins = [p, p, p, mu.reshape(1, WI), w2bd, w0.reshape(1, 2 * W), a2bd, a0.reshape(1, 2 * W), g2,
           k_k.reshape(1, W), k_a.reshape(1, W), r_k.reshape(1, W), ones_bd]
    rb = TT // SUBLANES
    in_specs = [pl.BlockSpec((1, TT, WI), lambda b, i: (b, i, 0)),
                pl.BlockSpec((1, SUBLANES, WI), lambda b, i: (b, jnp.maximum(i * rb - 1, 0), 0)),
                pl.BlockSpec((1, SUBLANES, WI), lambda b, i: (b, jnp.minimum((i + 1) * rb, nT * rb - 1), 0))]
    in_specs += [full(a) for a in ins[3:]]
    tok = jax.ShapeDtypeStruct((B, T, W), F32)
    tokz = jax.ShapeDtypeStruct((2, B, T, W), F32)
    s1 = pl.BlockSpec((1, TT, W), lambda b, i: (b, i, 0))
    s2 = pl.BlockSpec((2, 1, TT, W), lambda b, i: (0, b, i, 0))
    return pl.pallas_call(
        _prep_kernel,
        grid=(B, nT),
        in_specs=in_specs,
        out_specs=[s1, s1, s1, s2, s2, s2, s1, s1],
        out_shape=[tok, tok, tok, tokz, tokz, tokz, tok, tok],
        compiler_params=_params(("parallel", "parallel")),
        name="rwkv_prep",
    )(*ins)


def _sm(x, lo):
    return jnp.concatenate([jnp.where(lo, x, 0.0), jnp.where(lo, 0.0, x)], axis=0)


def _scan_chain(z, refs, hp, s_scr, y_ref, terms):
    r_ref, v_ref, a_ref, lw_ref, km_ref, bb_ref = refs
    C = CHUNK
    sl = slice(hp * LANES, (hp + 1) * LANES)
    sgn = 1 - 2 * z
    ri = lax.broadcasted_iota(jnp.int32, (2 * C, 2 * C), 0)
    ci = lax.broadcasted_iota(jnp.int32, (2 * C, 2 * C), 1)
    strict = (ri - ci) * sgn > 0
    incl = (ri - ci) * sgn >= 0
    eye = jnp.where(ri == ci, 1.0, 0.0)
    lo = lax.broadcasted_iota(jnp.int32, (C, LANES), 1) < RWKV_HEAD
    row = lax.broadcasted_iota(jnp.int32, (C, LANES), 0)

    lw = lw_ref[0, 0, :, sl]
    cum = lw
    for k in range(int(math.log2(C))):
        d = 1 << k
        if z == 0:
            cum = cum + jnp.where(row >= d, pltpu.roll(cum, d, 0), 0.0)
        else:
            cum = cum + jnp.where(row < C - d, pltpu.roll(cum, C - d, 0), 0.0)
    tot = jnp.sum(lw, axis=0, keepdims=True)
    dec_in = jnp.exp(cum - lw)
    dec_out = jnp.exp(cum)
    inv = jnp.exp(-cum)
    rest = jnp.exp(tot - cum)
    a_t = _sm(a_ref[0, :, sl] * dec_in, lo)
    r_t = _sm(r_ref[0, :, sl] * dec_out, lo)
    bb = bb_ref[0, 0, :, sl]
    km = km_ref[0, 0, :, sl]
    b_t = _sm(bb * inv, lo)
    k_t = _sm(km * inv, lo)
    bk_end = jnp.concatenate([_sm(bb * rest, lo), _sm(km * rest, lo)], axis=0)
    v_s = _sm(v_ref[0, :, sl], lo)
    ar = jnp.concatenate([a_t, r_t], axis=0)
    G = _mm(ar, jnp.concatenate([b_t, k_t], axis=0), dims=((1,), (1,)), na=terms[0], nb=terms[0])
    S = s_scr[z, hp]
    ms = _mm(ar, S, dims=((1,), (1,)), na=terms[2], nb=terms[2])
    yield
    L = jnp.where(strict, G[:2 * C, :2 * C], 0.0)
    Lak = jnp.where(strict, G[:2 * C, 2 * C:], 0.0)
    Grb = jnp.where(incl, G[2 * C:, :2 * C], 0.0)
    Grk = jnp.where(incl, G[2 * C:, 2 * C:], 0.0)
    X = ms[:2 * C] + _mm(Lak, v_s, na=terms[3], nb=terms[3])

    T = eye + jnp.where((ri >> 1) == (ci >> 1), L, 0.0)
    for lvl in range(1, int(math.log2(C))):
        off = ((ri >> (lvl + 1)) == (ci >> (lvl + 1))) & ((ri >> lvl) != (ci >> lvl))
        TL = _mm(T, jnp.where(off, L, 0.0), na=terms[1], nb=terms[1])
        yield
        T = T + _mm(TL, T, na=terms[1], nb=terms[1])
        yield

    U = _mm(T, X, na=terms[4][0], nb=terms[4][1])
    yield
    uv = jnp.concatenate([U, v_s], axis=0)
    Y = ms[2 * C:] + _mm(jnp.concatenate([Grb, Grk], axis=1), uv, na=terms[5], nb=terms[5])
    y_ref[0, :, sl] = Y[:C] + Y[C:]
    s_scr[z, hp] = S * jnp.exp(tot) + _mm(uv, bk_end, dims=((0,), (0,)), na=terms[6], nb=terms[6])


def _scan_kernel(rf, vf, af, lwf, kmf, bbf, rb, vb, ab, lwb, kmb, bbb, s0_ref, yf_ref, yb_ref, sN_ref, s_scr, *, terms):
    c = pl.program_id(1)

    @pl.when(c == 0)
    def _():
        s_scr[...] = s0_ref[:, 0]

    n_pairs = s_scr.shape[1]
    chains = [_scan_chain(0, (rf, vf, af, lwf, kmf, bbf), hp, s_scr, yf_ref, terms) for hp in range(n_pairs)]
    chains += [_scan_chain(1, (rb, vb, ab, lwb, kmb, bbb), hp, s_scr, yb_ref, terms) for hp in range(n_pairs)]
    for _ in zip(*chains):
        pass
    for ch in chains:
        for _ in ch:
            pass

    @pl.when(c == pl.num_programs(1) - 1)
    def _():
        sN_ref[:, 0] = s_scr[...]


SCAN_TERMS = (1, 1, 1, 1, (1, 2), 1, 1)


def _rwkv_scan(r, v, nkk, lw, km, bb, s0, terms=SCAN_TERMS):
    B, T, W = r.shape
    C = CHUNK
    NC = T // C
    NP = W // LANES
    fwd = lambda b, c: (b, c, 0)
    bwd = lambda b, c: (b, NC - 1 - c, 0)
    sh_f, sh_b = pl.BlockSpec((1, C, W), fwd), pl.BlockSpec((1, C, W), bwd)
    pd_f = pl.BlockSpec((1, 1, C, W), lambda b, c: (0, b, c, 0))
    pd_b = pl.BlockSpec((1, 1, C, W), lambda b, c: (1, b, NC - 1 - c, 0))
    state = pl.BlockSpec((2, 1, NP, LANES, LANES), lambda b, c: (0, b, 0, 0, 0))
    tok = jax.ShapeDtypeStruct((B, T, W), F32)
    return pl.pallas_call(
        functools.partial(_scan_kernel, terms=terms),
        grid=(B, NC),
        in_specs=[sh_f, sh_f, sh_f, pd_f, pd_f, pd_f, sh_b, sh_b, sh_b, pd_b, pd_b, pd_b, state],
        out_specs=[sh_f, sh_b, state],
        out_shape=[tok, tok, jax.ShapeDtypeStruct(s0.shape, F32)],
        scratch_shapes=[pltpu.VMEM((2, NP, LANES, LANES), F32)],
        compiler_params=_params(("parallel", "arbitrary")),
        name="rwkv_scan",
    )(r, v, nkk, lw, km, bb, r, v, nkk, lw, km, bb, s0)


def _rwkv_out_kernel(yf_ref, yb_ref, bv_ref, g_ref, lnw_ref, lnb_ref, ones_ref, o_ref):
    y = yf_ref[0] + yb_ref[0]
    ones_bd = ones_ref[...]
    inv_n = 1.0 / RWKV_HEAD
    mu = _segsum64(y, ones_bd) * inv_n
    d = y - mu
    var = _segsum64(d * d, ones_bd) * inv_n
    yn = d * lax.rsqrt(var + GN_EPS) * lnw_ref[...] + lnb_ref[...]
    o_ref[0] = ((yn + bv_ref[0]) * g_ref[0]).astype(o_ref.dtype)


def _rwkv_out(yf, yb, bv, g, ln_w, ln_b):
    B, T, W = yf.shape
    TT = _tile(T, 512, SUBLANES)
    ones_bd = _blockdiag2(jnp.ones((2, RWKV_HEAD, RWKV_HEAD), F32))
    s1 = pl.BlockSpec((1, TT, W), lambda b, i: (b, i, 0))
    return pl.pallas_call(
        _rwkv_out_kernel,
        grid=(B, T // TT),
        in_specs=[s1, s1, s1, s1,
                  pl.BlockSpec((1, W), lambda b, i: (0, 0)), pl.BlockSpec((1, W), lambda b, i: (0, 0)),
                  pl.BlockSpec((LANES, LANES), lambda b, i: (0, 0))],
        out_specs=s1,
        out_shape=jax.ShapeDtypeStruct((B, T, W), BF16),
        compiler_params=_params(("parallel", "parallel")),
        name="rwkv_out",
    )(yf, yb, bv, g, ln_w.reshape(1, W), ln_b.reshape(1, W), ones_bd)


def _merge_kernel(ao_ref, ro_ref, woa_ref, wor_ref, ga_ref, gr_ref, o_ref):
    ta = jnp.dot(ao_ref[...], woa_ref[...], preferred_element_type=F32)
    tr = jnp.dot(ro_ref[...], wor_ref[...], preferred_element_type=F32)
    o_ref[...] = (ga_ref[...].astype(F32) * ta + gr_ref[...].astype(F32) * tr).astype(o_ref.dtype)


def _merge(attn_o, rwkv_o, w_oa, w_or, gates):
    N, DA = attn_o.shape
    DR = rwkv_o.shape[1]
    D = w_oa.shape[1]
    TM = _tile(N, 1024, 16)
    TN = _tile(D, 512, LANES)
    nj = D // TN
    vmem = 2 * (TM * (DA + DR) * 2 + (DA + DR) * TN * 2 + 3 * TM * TN * 2) + 4 * TM * TN * 4
    return pl.pallas_call(
        _merge_kernel,
        grid=(N // TM, nj),
        in_specs=[pl.BlockSpec((TM, DA), lambda i, j: (i, 0)),
                  pl.BlockSpec((TM, DR), lambda i, j: (i, 0)),
                  pl.BlockSpec((DA, TN), lambda i, j: (0, j)),
                  pl.BlockSpec((DR, TN), lambda i, j: (0, j)),
                  pl.BlockSpec((TM, TN), lambda i, j: (i, j)),
                  pl.BlockSpec((TM, TN), lambda i, j: (i, nj + j))],
        out_specs=pl.BlockSpec((TM, TN), lambda i, j: (i, j)),
        out_shape=jax.ShapeDtypeStruct((N, D), BF16),
        compiler_params=_params(("parallel", "parallel"), vmem),
        name="merge",
    )(attn_o, rwkv_o, w_oa, w_or, gates, gates)


def _outproj_kernel(t_ref, w_ref, x_ref, mod_ref, o_ref, *, mod_base):
    acc = jnp.dot(t_ref[...], w_ref[...], preferred_element_type=F32)
    o_ref[...] = x_ref[...] + mod_ref[0, mod_base:mod_base + 1, :] * acc


def _outproj(t, w, x2, modtab, mod_row, mod_base):
    N, K = t.shape
    D = w.shape[1]
    seq, mod_row = mod_row
    TM = _tile(seq, 1024, 16)
    TN = _tile(D, 512, LANES)
    vmem = 2 * (TM * K * 2 + K * TN * 2 + 2 * TM * TN * 4) + 2 * TM * TN * 4
    return pl.pallas_call(
        functools.partial(_outproj_kernel, mod_base=mod_base),
        grid=(N // TM, D // TN),
        in_specs=[pl.BlockSpec((TM, K), lambda i, j: (i, 0)),
                  pl.BlockSpec((K, TN), lambda i, j: (0, j)),
                  pl.BlockSpec((TM, TN), lambda i, j: (i, j)),
                  pl.BlockSpec((1, N_MOD, TN), lambda i, j: (mod_row(i, TM), 0, j))],
        out_specs=pl.BlockSpec((TM, TN), lambda i, j: (i, j)),
        out_shape=jax.ShapeDtypeStruct((N, D), F32),
        compiler_params=_params(("parallel", "parallel"), vmem),
        name="outproj",
    )(t, w, x2, modtab)


def kernel(x, c, ctx, c_ctx, w_mod, b_mod, norm_ffn1, ffn1_w_in, ffn1_w_out, norm_mix, w_in, q_norm, k_norm, rwkv_mu, w0, w2, a0, a2, g2, k_k, k_a, r_k, ln_x_w, ln_x_b, w_oa, w_or, w_out, norm_ffn2, ffn2_w_in, ffn2_w_out, norm_final):
    B, S, D = x.shape
    CT = ctx.shape[1]
    assert w_mod.shape[0] == 1, "single layer"
    ATT_Q, ATT_KV = N_Q_HEADS * HEAD_DIM, N_KV_HEADS * HEAD_DIM
    assert w_in.shape[2] == ATT_Q + 2 * ATT_KV + RWKV_IN_W + 2 * D
    assert S % CHUNK == 0 and CT % CHUNK == 0 and S % GRID_W == 0

    lat_row = (S, lambda i, tm: (i * tm) // S)
    ctx_row = (B * CT, lambda i, tm: B)
    bf = lambda w: w.astype(BF16)

    modtab = _mod_table(c, c_ctx, w_mod[0], b_mod[0])

    x2 = x.reshape(B * S, D)
    cx2 = ctx.reshape(B * CT, D)
    w1_in, w1_out = bf(ffn1_w_in[0]), bf(ffn1_w_out[0])
    x2, h2 = _ffn(x2, modtab, lat_row, 0, norm_ffn1[0], w1_in, w1_out, next_norm=(norm_mix[0], 3))
    _, hc2 = _ffn(cx2, modtab, ctx_row, 0, norm_ffn1[0], w1_in, w1_out, next_norm=(norm_mix[0], 3))

    wi = w_in[0]
    o = 0
    wq = wi[:, o:o + ATT_Q]; o += ATT_Q
    wk = wi[:, o:o + ATT_KV]; o += ATT_KV
    wv = wi[:, o:o + ATT_KV]; o += ATT_KV
    wr = wi[:, o:o + RWKV_IN_W]; o += RWKV_IN_W
    wg = wi[:, o:]
    q_seg = ("headnorm_rope", ATT_Q, BF16, q_norm[0], HEAD_DIM ** -0.5 * LOG2E)
    v_seg = ("head_ones", ATT_KV, BF16, None, 1.0)
    r_seg = ("plain", RWKV_IN_W, F32, None, 1.0)
    g_seg = ("sigmoid", 2 * D, BF16, None, 1.0)
    q, k, v, p_rw, gates = _proj(h2, _proj_weight([wq, wk, wv, wr, wg]), S,
                                 [q_seg, ("headnorm_rope", ATT_KV, BF16, k_norm[0], 1.0), v_seg, r_seg, g_seg],
                                 rope_tables=_rope_tables(S))
    kc, vc, p_rw_c = _proj(hc2, _proj_weight([wk, wv, wr]), B * CT,
                           [("headnorm", ATT_KV, BF16, k_norm[0], 1.0), v_seg, r_seg])

    attn_o = _attention(q.reshape(B, S, ATT_Q), k.reshape(B, S, ATT_KV), v.reshape(B, S, 2 * ATT_KV),
                        kc.reshape(B, CT, ATT_KV), vc.reshape(B, CT, 2 * ATT_KV)).reshape(B * S, ATT_Q)

    rw = (rwkv_mu[0], w0[0], w2[0], a0[0], a2[0], g2[0], k_k[0], k_a[0], r_k[0])
    rc, vcr, akc, lwc, kmc, bbc, _, _ = _rwkv_prep(p_rw_c.reshape(B, CT, -1), *rw)
    s0 = jnp.zeros((2, B, RWKV_W // LANES, LANES, LANES), F32)
    _, _, s_ctx = _rwkv_scan(rc, vcr, akc, lwc, kmc, bbc, s0)
    rl, vl, akl, lwl, kml, bbl, gl, bvl = _rwkv_prep(p_rw.reshape(B, S, -1), *rw)
    yf, yb, _ = _rwkv_scan(rl, vl, akl, lwl, kml, bbl, s_ctx)
    rwkv_o = _rwkv_out(yf, yb, bvl, gl, ln_x_w[0], ln_x_b[0]).reshape(B * S, RWKV_W)

    t = _merge(attn_o, rwkv_o, bf(w_oa[0]), bf(w_or[0]), gates)
    x2 = _outproj(t, bf(w_out[0]), x2, modtab, lat_row, 5)

    out = _ffn(x2, modtab, lat_row, 6, norm_ffn2[0], bf(ffn2_w_in[0]), bf(ffn2_w_out[0]), g_final=norm_final)
    return out.reshape(B, S, D)
```

```python
import functools
import math

import jax
import jax.numpy as jnp
from jax import lax
from jax.experimental import pallas as pl
from jax.experimental.pallas import tpu as pltpu

F32 = jnp.float32
BF16 = jnp.bfloat16

N_Q_HEADS = 16
N_KV_HEADS = 4
HEAD_DIM = 128
Q_PER_KV = N_Q_HEADS // N_KV_HEADS
RWKV_HEADS = 16
RWKV_HEAD = 64
RWKV_W = RWKV_HEADS * RWKV_HEAD
W_LORA = 64
A_LORA = 64
G_LORA = 128
RWKV_IN_W = 3 * RWKV_W + 2 * W_LORA + 2 * A_LORA + G_LORA
GRID_W = 64
ROPE_THETA = 10000.0
N_MOD = 9
EPS = 1e-6
GN_EPS = 64e-5
LOG2E = math.log2(math.e)
DECAY_RATE = math.exp(-0.5)

LANES = 128
SUBLANES = 8
VMEM_BYTES_V7X = 64 * 1024 * 1024
VMEM_LIMIT = 60000 * 1024

CHUNK = 64
PAIR = 2 * RWKV_HEAD
assert PAIR == LANES


def _tile(n, pref, mult):
    if n <= pref:
        return n
    best = None
    for t in range(mult, pref + 1, mult):
        if n % t == 0:
            best = t
    assert best is not None, (n, pref, mult)
    return best


def _params(sem, vmem=None):
    return pltpu.CompilerParams(dimension_semantics=sem, vmem_limit_bytes=min(vmem or VMEM_LIMIT, VMEM_LIMIT))


def _split_bf16(x, n):
    terms, rem = [], x
    for _ in range(n):
        t = rem.astype(BF16)
        terms.append(t)
        rem = rem - t.astype(F32)
    return terms


def _dotg(a, b, dims):
    return lax.dot_general(a, b, (dims, ((), ())), preferred_element_type=F32)


def _mm(a, b, dims=((1,), (0,)), na=1, nb=1):
    at = _split_bf16(a, na) if a.dtype != BF16 else [a]
    if isinstance(b, (list, tuple)):
        bt = list(b)
    else:
        bt = _split_bf16(b, nb) if b.dtype != BF16 else [b]
    order = max(len(at), len(bt))
    acc = None
    for i, x in enumerate(at):
        for j, y in enumerate(bt):
            if i + j < order:
                d = _dotg(x, y, dims)
                acc = d if acc is None else acc + d
    return acc


def _rmsnorm(x, g):
    ms = jnp.mean(x * x, axis=-1, keepdims=True)
    return x * lax.rsqrt(ms + EPS) * g


def _mod_kernel(c_ref, w_ref, b_ref, o_ref):
    c = c_ref[...]
    s = (c * jax.nn.sigmoid(c)).astype(BF16)
    o_ref[...] = jnp.dot(s, w_ref[...].astype(BF16), preferred_element_type=F32) + b_ref[...]


def _mod_table(c, c_ctx, w_mod, b_mod):
    B, D = c.shape
    NO = w_mod.shape[1]
    rows = -(-(B + 1) // SUBLANES) * SUBLANES
    cin = jnp.zeros((rows, D), F32).at[:B].set(c).at[B].set(c_ctx)
    TN = _tile(NO, 1024, LANES)
    out = pl.pallas_call(
        _mod_kernel,
        grid=(NO // TN,),
        in_specs=[pl.BlockSpec((rows, D), lambda j: (0, 0)),
                  pl.BlockSpec((D, TN), lambda j: (0, j)),
                  pl.BlockSpec((1, TN), lambda j: (0, j))],
        out_specs=pl.BlockSpec((rows, TN), lambda j: (0, j)),
        out_shape=jax.ShapeDtypeStruct((rows, NO), F32),
        compiler_params=_params(("parallel",)),
        name="mod_table",
    )(cin, w_mod, b_mod.reshape(1, NO))
    return out[:B + 1].reshape(B + 1, N_MOD, D)


def _ffn_kernel(x_ref, mod_ref, g_ref, wg_ref, wu_ref, wo_ref, *rest, mod_base, final_norm, next_base):
    gf_ref = gn_ref = hn_ref = None
    if final_norm:
        gf_ref, o_ref, h_scr, acc_scr = rest
    elif next_base is not None:
        gn_ref, o_ref, hn_ref, h_scr, acc_scr = rest
    else:
        o_ref, h_scr, acc_scr = rest
    f = pl.program_id(1)
    last = pl.num_programs(1) - 1
    TM = x_ref.shape[0]
    halves = [pl.ds(0, TM // 2), pl.ds(TM // 2, TM // 2)] if TM % 32 == 0 else [pl.ds(0, TM)]

    def prologue(rows):
        shift = mod_ref[0, mod_base:mod_base + 1, :]
        scale = mod_ref[0, mod_base + 1:mod_base + 2, :]
        h = _rmsnorm(x_ref[rows, :], g_ref[...]) * (1.0 + scale) + shift
        h_scr[rows, :] = h.astype(BF16)

    def chunk(rows, first):
        h = h_scr[rows, :]
        gt = jnp.dot(h, wg_ref[...], preferred_element_type=F32)
        up = jnp.dot(h, wu_ref[...], preferred_element_type=F32)
        act = (gt * jax.nn.sigmoid(gt) * up).astype(BF16)
        part = jnp.dot(act, wo_ref[...], preferred_element_type=F32)
        acc_scr[rows, :] = part if first else acc_scr[rows, :] + part

    def epilogue(rows):
        gate = mod_ref[0, mod_base + 2:mod_base + 3, :]
        out = x_ref[rows, :] + 0.5 * gate * acc_scr[rows, :]
        if final_norm:
            out = _rmsnorm(out, gf_ref[...])
        o_ref[rows, :] = out
        if next_base is not None:
            shift = mod_ref[0, next_base:next_base + 1, :]
            scale = mod_ref[0, next_base + 1:next_base + 2, :]
            hn_ref[rows, :] = (_rmsnorm(out, gn_ref[...]) * (1.0 + scale) + shift).astype(BF16)

    @pl.when(f == 0)
    def _():
        for rows in halves:
            prologue(rows)
            chunk(rows, first=True)

    @pl.when((f > 0) & (f < last))
    def _():
        chunk(pl.ds(0, TM), first=False)

    @pl.when(f == last)
    def _():
        for rows in halves:
            chunk(rows, first=False)
            epilogue(rows)


def _ffn(x2, modtab, mod_row, mod_base, g, w_in, w_out, g_final=None, next_norm=None):
    N, D = x2.shape
    F = w_out.shape[0]
    seq, mod_row = mod_row
    TM = _tile(seq, 512, SUBLANES)
    TF = _tile(F, 512, LANES)
    nf = F // TF
    assert nf >= 2, "first and last d_ff chunk are distinct grid steps"
    in_specs = [pl.BlockSpec((TM, D), lambda i, f: (i, 0)),
                pl.BlockSpec((1, N_MOD, D), lambda i, f: (mod_row(i, TM), 0, 0)),
                pl.BlockSpec((1, D), lambda i, f: (0, 0)),
                pl.BlockSpec((D, TF), lambda i, f: (0, f)),
                pl.BlockSpec((D, TF), lambda i, f: (0, nf + f)),
                pl.BlockSpec((TF, D), lambda i, f: (f, 0))]
    args = [x2, modtab, g.reshape(1, D), w_in, w_in, w_out]
    assert g_final is None or next_norm is None
    row_spec = pl.BlockSpec((TM, D), lambda i, f: (i, 0))
    out_specs, out_shape = row_spec, jax.ShapeDtypeStruct((N, D), F32)
    if g_final is not None or next_norm is not None:
        in_specs.append(pl.BlockSpec((1, D), lambda i, f: (0, 0)))
        args.append((g_final if g_final is not None else next_norm[0]).reshape(1, D))
    if next_norm is not None:
        out_specs, out_shape = [row_spec, row_spec], [out_shape, jax.ShapeDtypeStruct((N, D), BF16)]
    vmem = 2 * (2 * TM * D * 4 + TM * D * 2 + 3 * D * TF * 2) + TM * D * 6 + 4 * TM * TF * 4
    return pl.pallas_call(
        functools.partial(_ffn_kernel, mod_base=mod_base, final_norm=g_final is not None,
                          next_base=None if next_norm is None else next_norm[1]),
        grid=(N // TM, nf),
        in_specs=in_specs,
        out_specs=out_specs,
        out_shape=out_shape,
        scratch_shapes=[pltpu.VMEM((TM, D), BF16), pltpu.VMEM((TM, D), F32)],
        compiler_params=_params(("parallel", "arbitrary"), vmem + (8 << 20)),
        name="ffn",
    )(*args)


def _swap32(t):
    lane = lax.broadcasted_iota(jnp.int32, t.shape, t.ndim - 1)
    return jnp.where((lane % 64) >= 32, pltpu.roll(t, 32, t.ndim - 1), pltpu.roll(t, LANES - 32, t.ndim - 1))


PROJ_TN = 512


def _proj_kernel(h_ref, w_ref, *rest, segs, rope):
    n_extra = len(segs) + (2 if rope else 0)
    extras, outs = rest[:n_extra], rest[n_extra:]
    j = pl.program_id(1)
    TM = h_ref.shape[0]
    n_split = 2 if TM % 32 == 0 else 1
    t0 = 0
    for si, (kind, n_tiles, scale) in enumerate(segs):
        def tile(kind=kind, scale=scale, o_ref=outs[si], hn_ref=extras[si]):
            for rows in (pl.ds(r * (TM // n_split), TM // n_split) for r in range(n_split)):
                acc = jnp.dot(h_ref[rows, :], w_ref[...], preferred_element_type=F32)
                if kind == "plain":
                    o_ref[rows, :] = acc.astype(o_ref.dtype)
                elif kind == "head_ones":
                    for hd in range(acc.shape[1] // HEAD_DIM):
                        o_ref[rows, 2 * hd * HEAD_DIM:(2 * hd + 1) * HEAD_DIM] = (
                            acc[:, hd * HEAD_DIM:(hd + 1) * HEAD_DIM].astype(o_ref.dtype))
                        o_ref[rows, (2 * hd + 1) * HEAD_DIM:(2 * hd + 2) * HEAD_DIM] = (
                            jnp.ones((acc.shape[0], HEAD_DIM), o_ref.dtype))
                elif kind == "sigmoid":
                    o_ref[rows, :] = (0.5 * jnp.tanh(0.5 * acc) + 0.5).astype(o_ref.dtype)
                else:
                    ones = jnp.ones((HEAD_DIM, HEAD_DIM), BF16)
                    for hd in range(acc.shape[1] // HEAD_DIM):
                        t = acc[:, hd * HEAD_DIM:(hd + 1) * HEAD_DIM]
                        ms = _mm(t * t, ones, na=1, nb=1) * (1.0 / HEAD_DIM)
                        t = t * lax.rsqrt(ms + EPS) * hn_ref[...]
                        if kind == "headnorm_rope":
                            t = t * extras[-2][rows, :] + _swap32(t) * extras[-1][rows, :]
                        o_ref[rows, hd * HEAD_DIM:(hd + 1) * HEAD_DIM] = (t * scale).astype(o_ref.dtype)
        pl.when((j >= t0) & (j < t0 + n_tiles))(tile)
        t0 += n_tiles


def _proj(h2, w, seq, segs, rope_tables=None):
    N, D = h2.shape
    TN = PROJ_TN
    TM = _tile(seq, 1024, 16)
    tiles = [-(-width // TN) for _, width, _, _, _ in segs]
    assert w.shape[1] == TN * sum(tiles)
    dummy = jnp.zeros((1, HEAD_DIM), F32)
    extras = [dummy if hn is None else hn.reshape(1, HEAD_DIM) for _, _, _, hn, _ in segs]
    in_specs = [pl.BlockSpec((TM, D), lambda i, j: (i, 0)), pl.BlockSpec((D, TN), lambda i, j: (0, j))]
    in_specs += [pl.BlockSpec((1, HEAD_DIM), lambda i, j: (0, 0))] * len(segs)
    if rope_tables is not None:
        extras += list(rope_tables)
        in_specs += [pl.BlockSpec((TM, HEAD_DIM), lambda i, j: (i % (seq // TM), 0))] * 2
    out_specs, out_shape, t0 = [], [], 0
    for (kind, width, dtype, _, _), nt in zip(segs, tiles):
        ow = 2 * TN if kind == "head_ones" else TN
        out_specs.append(pl.BlockSpec((TM, ow), lambda i, j, t0=t0, nt=nt: (i, jnp.clip(j - t0, 0, nt - 1))))
        out_shape.append(jax.ShapeDtypeStruct((N, nt * ow), dtype))
        t0 += nt
    vmem = 2 * (TM * D * 2 + D * TN * 2 + len(segs) * TM * TN * 4 + 2 * TM * LANES * 4) + 6 * TM * TN * 4
    return pl.pallas_call(
        functools.partial(_proj_kernel, segs=tuple((k, nt, sc) for (k, _, _, _, sc), nt in zip(segs, tiles)),
                          rope=rope_tables is not None),
        grid=(N // TM, sum(tiles)),
        in_specs=in_specs,
        out_specs=out_specs,
        out_shape=out_shape,
        compiler_params=_params(("parallel", "arbitrary"), vmem),
        name="proj",
    )(h2, w, *extras)


def _proj_weight(w_cols):
    pad = lambda w: jnp.pad(w.astype(BF16), ((0, 0), (0, -w.shape[1] % PROJ_TN)))
    return jnp.concatenate([pad(w) for w in w_cols], axis=1)


def _rope_tables(n_tok):
    rows = n_tok // GRID_W
    row = jnp.repeat(jnp.arange(rows, dtype=jnp.int32), GRID_W).astype(F32)
    col = jnp.tile(jnp.arange(GRID_W, dtype=jnp.int32), rows).astype(F32)
    axis_dim = HEAD_DIM // 2
    inv = ROPE_THETA ** (-jnp.arange(0, axis_dim, 2, dtype=F32) / axis_dim)
    ang_r = row[:, None] * inv[None]
    ang_c = col[:, None] * inv[None]
    cr, sr, cc, sc = jnp.cos(ang_r), jnp.sin(ang_r), jnp.cos(ang_c), jnp.sin(ang_c)
    return (jnp.concatenate([cr, cr, cc, cc], axis=-1), jnp.concatenate([-sr, sr, -sc, sc], axis=-1))


ATTN_TK = 768


def _attn_kernel(q_ref, k_ref, v_ref, kc_ref, vc_ref, o_ref, *, TQ):
    S = k_ref.shape[1]
    n_full = (S - 1) // ATTN_TK
    tail = n_full * ATTN_TK

    def kv(j):
        if j < n_full:
            rows = slice(j * ATTN_TK, (j + 1) * ATTN_TK)
            return k_ref[0, rows, :], v_ref[0, rows, :]
        return (jnp.concatenate([k_ref[0, tail:, :], kc_ref[0]], axis=0),
                jnp.concatenate([v_ref[0, tail:, :], vc_ref[0]], axis=0))

    q = jnp.concatenate([q_ref[0, :, h * HEAD_DIM:(h + 1) * HEAD_DIM] for h in range(Q_PER_KV)], axis=0)
    R = q.shape[0]
    tiles = [kv(j) for j in range(n_full + 1)]
    scores = lambda j: _dotg(q, tiles[j][0], ((1,), (1,)))
    m = jnp.full((R, 1), -jnp.inf, F32)
    acc = jnp.zeros((R, 2 * HEAD_DIM), F32)
    s_next = scores(0)
    for j in range(n_full + 1):
        s = s_next
        if j < n_full:
            s_next = scores(j + 1)
        m_new = jnp.maximum(m, jnp.max(s, axis=-1, keepdims=True))
        p = jnp.exp2(s - m_new).astype(BF16)
        acc = jnp.exp2(m - m_new) * acc + jnp.dot(p, tiles[j][1], preferred_element_type=F32)
        m = m_new
    o = acc[:, :HEAD_DIM] / acc[:, HEAD_DIM:]
    for h in range(Q_PER_KV):
        o_ref[0, :, h * HEAD_DIM:(h + 1) * HEAD_DIM] = o[h * TQ:(h + 1) * TQ, :].astype(o_ref.dtype)


def _attention(q, k, v, kc, vc):
    B, S, _ = q.shape
    CT = kc.shape[1]
    TQ = _tile(S, 256, 16)
    GW = Q_PER_KV * HEAD_DIM
    R = Q_PER_KV * TQ
    vmem = 2 * (2 * TQ * GW * 2 + 3 * (S + CT) * HEAD_DIM * 2) + R * HEAD_DIM * (2 + 4 * 3) + 10 * R * ATTN_TK * 4
    kv_spec = lambda rows, width: pl.BlockSpec((1, rows, width), lambda b, g, qi: (b, 0, g))
    return pl.pallas_call(
        functools.partial(_attn_kernel, TQ=TQ),
        grid=(B, N_KV_HEADS, S // TQ),
        in_specs=[pl.BlockSpec((1, TQ, GW), lambda b, g, qi: (b, qi, g)),
                  kv_spec(S, HEAD_DIM), kv_spec(S, 2 * HEAD_DIM), kv_spec(CT, HEAD_DIM), kv_spec(CT, 2 * HEAD_DIM)],
        out_specs=pl.BlockSpec((1, TQ, GW), lambda b, g, qi: (b, qi, g)),
        out_shape=jax.ShapeDtypeStruct(q.shape, BF16),
        compiler_params=_params(("parallel", "parallel", "arbitrary"), vmem),
        name="gqa_flash",
    )(q, k, v, kc, vc)


def _segsum64(x, ones_bd):
    cols = []
    for c in range(x.shape[1] // LANES):
        cols.append(_mm(x[:, c * LANES:(c + 1) * LANES], ones_bd, na=2, nb=1))
    return jnp.concatenate(cols, axis=1)


def _prep_kernel(p_ref, pprev_ref, pnext_ref, mu_ref, w2_ref, w0_ref, a2_ref, a0_ref, g2_ref, kk_ref, ka_ref, rk_ref,
                 ones_ref, r_o, v_o, nkk_o, lw_o, km_o, bb_o, g_o, bv_o):
    p = p_ref[0]
    TT = p.shape[0]
    row = lax.broadcasted_iota(jnp.int32, (TT, 1), 0)
    i = pl.program_id(1)
    before = jnp.where(i == 0, 0.0, pprev_ref[0, SUBLANES - 1:SUBLANES, :])
    after = jnp.where(i == pl.num_programs(1) - 1, 0.0, pnext_ref[0, 0:1, :])
    prev = jnp.where(row == 0, before, pltpu.roll(p, 1, 0))
    nxt = jnp.where(row == TT - 1, after, pltpu.roll(p, TT - 1, 0))
    ps = p + mu_ref[...] * (0.5 * (prev + nxt) - p)
    W = RWKV_W
    r, k, v = ps[:, 0:W], ps[:, W:2 * W], ps[:, 2 * W:3 * W]
    lw = ps[:, 3 * W:3 * W + 2 * W_LORA]
    la = ps[:, 3 * W + 2 * W_LORA:3 * W + 2 * W_LORA + 2 * A_LORA]
    lg = ps[:, 3 * W + 2 * W_LORA + 2 * A_LORA:]
    ones_bd = ones_ref[...]
    terms = lambda ref: (ref[0], ref[1])
    w_raw = _mm(jnp.tanh(lw), terms(w2_ref), na=2) + w0_ref[...]
    logw = -DECAY_RATE * jax.nn.sigmoid(w_raw)
    a = jax.nn.sigmoid(_mm(la, terms(a2_ref), na=2) + a0_ref[...])
    g = _mm(jax.nn.sigmoid(lg), terms(g2_ref), na=2)
    kkv = k * kk_ref[...]
    kk = kkv * jnp.minimum(lax.rsqrt(_segsum64(kkv * kkv, ones_bd)), 1e12)
    ka = ka_ref[...]
    kmod_sum = None
    for z in range(2):
        az = a[:, z * W:(z + 1) * W]
        kmod = k * (1.0 + (az - 1.0) * ka)
        lw_o[z, 0] = logw[:, z * W:(z + 1) * W]
        km_o[z, 0] = kmod
        bb_o[z, 0] = kk * az
        kmod_sum = kmod if kmod_sum is None else kmod_sum + kmod
    bonus = _segsum64(r * kmod_sum * rk_ref[...], ones_bd)
    r_o[0] = r
    v_o[0] = v
    nkk_o[0] = -kk
    g_o[0] = g
    bv_o[0] = bonus * v


def _blockdiag2(m):
    _, R, C = m.shape
    z = jnp.zeros((R, C), m.dtype)
    return jnp.concatenate([jnp.concatenate([m[0], z], axis=1), jnp.concatenate([z, m[1]], axis=1)], axis=0)


def _rwkv_prep(p, mu, w0, w2, a0, a2, g2, k_k, k_a, r_k):
    B, T, _ = p.shape
    WI = RWKV_IN_W
    TT = _tile(T, 256, SUBLANES)
    nT = T // TT
    W = RWKV_W
    ones_bd = _blockdiag2(jnp.ones((2, RWKV_HEAD, RWKV_HEAD), F32))
    full = lambda a: pl.BlockSpec(a.shape, lambda b, i: (0,) * a.ndim)
    split2 = lambda w: jnp.stack(_split_bf16(w, 2))
    w2bd, a2bd, g2 = split2(_blockdiag2(w2)), split2(_blockdiag2(a2)), split2(g2)
    ins = [p, p, p, mu.reshape(1, WI), w2bd, w0.reshape(1, 2 * W), a2bd, a0.reshape(1, 2 * W), g2,
           k_k.reshape(1, W), k_a.reshape(1, W), r_k.reshape(1, W), ones_bd]
    rb = TT // SUBLANES
    in_specs = [pl.BlockSpec((1, TT, WI), lambda b, i: (b, i, 0)),
                pl.BlockSpec((1, SUBLANES, WI), lambda b, i: (b, jnp.maximum(i * rb - 1, 0), 0)),
                pl.BlockSpec((1, SUBLANES, WI), lambda b, i: (b, jnp.minimum((i + 1) * rb, nT * rb - 1), 0))]
    in_specs += [full(a) for a in ins[3:]]
    tok = jax.ShapeDtypeStruct((B, T, W), F32)
    tokz = jax.ShapeDtypeStruct((2, B, T, W), F32)
    s1 = pl.BlockSpec((1, TT, W), lambda b, i: (b, i, 0))
    s2 = pl.BlockSpec((2, 1, TT, W), lambda b, i: (0, b, i, 0))
    return pl.pallas_call(
        _prep_kernel,
        grid=(B, nT),
        in_specs=in_specs,
        out_specs=[s1, s1, s1, s2, s2, s2, s1, s1],
        out_shape=[tok, tok, tok, tokz, tokz, tokz, tok, tok],
        compiler_params=_params(("parallel", "parallel")),
        name="rwkv_prep",
    )(*ins)


def _sm(x, lo):
    return jnp.concatenate([jnp.where(lo, x, 0.0), jnp.where(lo, 0.0, x)], axis=0)


def _scan_chain(z, refs, hp, s_scr, y_ref, terms):
    r_ref, v_ref, a_ref, lw_ref, km_ref, bb_ref = refs
    C = CHUNK
    sl = slice(hp * LANES, (hp + 1) * LANES)
    sgn = 1 - 2 * z
    ri = lax.broadcasted_iota(jnp.int32, (2 * C, 2 * C), 0)
    ci = lax.broadcasted_iota(jnp.int32, (2 * C, 2 * C), 1)
    strict = (ri - ci) * sgn > 0
    incl = (ri - ci) * sgn >= 0
    eye = jnp.where(ri == ci, 1.0, 0.0)
    lo = lax.broadcasted_iota(jnp.int32, (C, LANES), 1) < RWKV_HEAD
    row = lax.broadcasted_iota(jnp.int32, (C, LANES), 0)

    lw = lw_ref[0, 0, :, sl]
    cum = lw
    for k in range(int(math.log2(C))):
        d = 1 << k
        if z == 0:
            cum = cum + jnp.where(row >= d, pltpu.roll(cum, d, 0), 0.0)
        else:
            cum = cum + jnp.where(row < C - d, pltpu.roll(cum, C - d, 0), 0.0)
    tot = jnp.sum(lw, axis=0, keepdims=True)
    dec_in = jnp.exp(cum - lw)
    dec_out = jnp.exp(cum)
    inv = jnp.exp(-cum)
    rest = jnp.exp(tot - cum)
    a_t = _sm(a_ref[0, :, sl] * dec_in, lo)
    r_t = _sm(r_ref[0, :, sl] * dec_out, lo)
    bb = bb_ref[0, 0, :, sl]
    km = km_ref[0, 0, :, sl]
    b_t = _sm(bb * inv, lo)
    k_t = _sm(km * inv, lo)
    bk_end = jnp.concatenate([_sm(bb * rest, lo), _sm(km * rest, lo)], axis=0)
    v_s = _sm(v_ref[0, :, sl], lo)
    ar = jnp.concatenate([a_t, r_t], axis=0)
    G = _mm(ar, jnp.concatenate([b_t, k_t], axis=0), dims=((1,), (1,)), na=terms[0], nb=terms[0])
    S = s_scr[z, hp]
    ms = _mm(ar, S, dims=((1,), (1,)), na=terms[2], nb=terms[2])
    yield
    L = jnp.where(strict, G[:2 * C, :2 * C], 0.0)
    Lak = jnp.where(strict, G[:2 * C, 2 * C:], 0.0)
    Grb = jnp.where(incl, G[2 * C:, :2 * C], 0.0)
    Grk = jnp.where(incl, G[2 * C:, 2 * C:], 0.0)
    X = ms[:2 * C] + _mm(Lak, v_s, na=terms[3], nb=terms[3])

    T = eye + jnp.where((ri >> 1) == (ci >> 1), L, 0.0)
    for lvl in range(1, int(math.log2(C))):
        off = ((ri >> (lvl + 1)) == (ci >> (lvl + 1))) & ((ri >> lvl) != (ci >> lvl))
        TL = _mm(T, jnp.where(off, L, 0.0), na=terms[1], nb=terms[1])
        yield
        T = T + _mm(TL, T, na=terms[1], nb=terms[1])
        yield

    U = _mm(T, X, na=terms[4][0], nb=terms[4][1])
    yield
    uv = jnp.concatenate([U, v_s], axis=0)
    Y = ms[2 * C:] + _mm(jnp.concatenate([Grb, Grk], axis=1), uv, na=terms[5], nb=terms[5])
    y_ref[0, :, sl] = Y[:C] + Y[C:]
    s_scr[z, hp] = S * jnp.exp(tot) + _mm(uv, bk_end, dims=((0,), (0,)), na=terms[6], nb=terms[6])


def _scan_kernel(rf, vf, af, lwf, kmf, bbf, rb, vb, ab, lwb, kmb, bbb, s0_ref, yf_ref, yb_ref, sN_ref, s_scr, *, terms):
    c = pl.program_id(1)

    @pl.when(c == 0)
    def _():
        s_scr[...] = s0_ref[:, 0]

    n_pairs = s_scr.shape[1]
    chains = [_scan_chain(0, (rf, vf, af, lwf, kmf, bbf), hp, s_scr, yf_ref, terms) for hp in range(n_pairs)]
    chains += [_scan_chain(1, (rb, vb, ab, lwb, kmb, bbb), hp, s_scr, yb_ref, terms) for hp in range(n_pairs)]
    for _ in zip(*chains):
        pass
    for ch in chains:
        for _ in ch:
            pass

    @pl.when(c == pl.num_programs(1) - 1)
    def _():
        sN_ref[:, 0] = s_scr[...]


SCAN_TERMS = (1, 1, 1, 1, (1, 1), 1, 1)


def _rwkv_scan(r, v, nkk, lw, km, bb, s0, terms=SCAN_TERMS):
    B, T, W = r.shape
    C = CHUNK
    NC = T // C
    NP = W // LANES
    fwd = lambda b, c: (b, c, 0)
    bwd = lambda b, c: (b, NC - 1 - c, 0)
    sh_f, sh_b = pl.BlockSpec((1, C, W), fwd), pl.BlockSpec((1, C, W), bwd)
    pd_f = pl.BlockSpec((1, 1, C, W), lambda b, c: (0, b, c, 0))
    pd_b = pl.BlockSpec((1, 1, C, W), lambda b, c: (1, b, NC - 1 - c, 0))
    state = pl.BlockSpec((2, 1, NP, LANES, LANES), lambda b, c: (0, b, 0, 0, 0))
    tok = jax.ShapeDtypeStruct((B, T, W), F32)
    return pl.pallas_call(
        functools.partial(_scan_kernel, terms=terms),
        grid=(B, NC),
        in_specs=[sh_f, sh_f, sh_f, pd_f, pd_f, pd_f, sh_b, sh_b, sh_b, pd_b, pd_b, pd_b, state],
        out_specs=[sh_f, sh_b, state],
        out_shape=[tok, tok, jax.ShapeDtypeStruct(s0.shape, F32)],
        scratch_shapes=[pltpu.VMEM((2, NP, LANES, LANES), F32)],
        compiler_params=_params(("parallel", "arbitrary")),
        name="rwkv_scan",
    )(r, v, nkk, lw, km, bb, r, v, nkk, lw, km, bb, s0)


def _rwkv_out_kernel(yf_ref, yb_ref, bv_ref, g_ref, lnw_ref, lnb_ref, ones_ref, o_ref):
    y = yf_ref[0] + yb_ref[0]
    ones_bd = ones_ref[...]
    inv_n = 1.0 / RWKV_HEAD
    mu = _segsum64(y, ones_bd) * inv_n
    d = y - mu
    var = _segsum64(d * d, ones_bd) * inv_n
    yn = d * lax.rsqrt(var + GN_EPS) * lnw_ref[...] + lnb_ref[...]
    o_ref[0] = ((yn + bv_ref[0]) * g_ref[0]).astype(o_ref.dtype)


def _rwkv_out(yf, yb, bv, g, ln_w, ln_b):
    B, T, W = yf.shape
    TT = _tile(T, 512, SUBLANES)
    ones_bd = _blockdiag2(jnp.ones((2, RWKV_HEAD, RWKV_HEAD), F32))
    s1 = pl.BlockSpec((1, TT, W), lambda b, i: (b, i, 0))
    return pl.pallas_call(
        _rwkv_out_kernel,
        grid=(B, T // TT),
        in_specs=[s1, s1, s1, s1,
                  pl.BlockSpec((1, W), lambda b, i: (0, 0)), pl.BlockSpec((1, W), lambda b, i: (0, 0)),
                  pl.BlockSpec((LANES, LANES), lambda b, i: (0, 0))],
        out_specs=s1,
        out_shape=jax.ShapeDtypeStruct((B, T, W), BF16),
        compiler_params=_params(("parallel", "parallel")),
        name="rwkv_out",
    )(yf, yb, bv, g, ln_w.reshape(1, W), ln_b.reshape(1, W), ones_bd)


def _merge_kernel(ao_ref, ro_ref, woa_ref, wor_ref, ga_ref, gr_ref, o_ref):
    ta = jnp.dot(ao_ref[...], woa_ref[...], preferred_element_type=F32)
    tr = jnp.dot(ro_ref[...], wor_ref[...], preferred_element_type=F32)
    o_ref[...] = (ga_ref[...].astype(F32) * ta + gr_ref[...].astype(F32) * tr).astype(o_ref.dtype)


def _merge(attn_o, rwkv_o, w_oa, w_or, gates):
    N, DA = attn_o.shape
    DR = rwkv_o.shape[1]
    D = w_oa.shape[1]
    TM = _tile(N, 1024, 16)
    TN = _tile(D, 512, LANES)
    nj = D // TN
    vmem = 2 * (TM * (DA + DR) * 2 + (DA + DR) * TN * 2 + 3 * TM * TN * 2) + 4 * TM * TN * 4
    return pl.pallas_call(
        _merge_kernel,
        grid=(N // TM, nj),
        in_specs=[pl.BlockSpec((TM, DA), lambda i, j: (i, 0)),
                  pl.BlockSpec((TM, DR), lambda i, j: (i, 0)),
                  pl.BlockSpec((DA, TN), lambda i, j: (0, j)),
                  pl.BlockSpec((DR, TN), lambda i, j: (0, j)),
                  pl.BlockSpec((TM, TN), lambda i, j: (i, j)),
                  pl.BlockSpec((TM, TN), lambda i, j: (i, nj + j))],
        out_specs=pl.BlockSpec((TM, TN), lambda i, j: (i, j)),
        out_shape=jax.ShapeDtypeStruct((N, D), BF16),
        compiler_params=_params(("parallel", "parallel"), vmem),
        name="merge",
    )(attn_o, rwkv_o, w_oa, w_or, gates, gates)


def _outproj_kernel(t_ref, w_ref, x_ref, mod_ref, o_ref, *, mod_base):
    acc = jnp.dot(t_ref[...], w_ref[...], preferred_element_type=F32)
    o_ref[...] = x_ref[...] + mod_ref[0, mod_base:mod_base + 1, :] * acc


def _outproj(t, w, x2, modtab, mod_row, mod_base):
    N, K = t.shape
    D = w.shape[1]
    seq, mod_row = mod_row
    TM = _tile(seq, 1024, 16)
    TN = _tile(D, 512, LANES)
    vmem = 2 * (TM * K * 2 + K * TN * 2 + 2 * TM * TN * 4) + 2 * TM * TN * 4
    return pl.pallas_call(
        functools.partial(_outproj_kernel, mod_base=mod_base),
        grid=(N // TM, D // TN),
        in_specs=[pl.BlockSpec((TM, K), lambda i, j: (i, 0)),
                  pl.BlockSpec((K, TN), lambda i, j: (0, j)),
                  pl.BlockSpec((TM, TN), lambda i, j: (i, j)),
                  pl.BlockSpec((1, N_MOD, TN), lambda i, j: (mod_row(i, TM), 0, j))],
        out_specs=pl.BlockSpec((TM, TN), lambda i, j: (i, j)),
        out_shape=jax.ShapeDtypeStruct((N, D), F32),
        compiler_params=_params(("parallel", "parallel"), vmem),
        name="outproj",
    )(t, w, x2, modtab)


def kernel(x, c, ctx, c_ctx, w_mod, b_mod, norm_ffn1, ffn1_w_in, ffn1_w_out, norm_mix, w_in, q_norm, k_norm, rwkv_mu, w0, w2, a0, a2, g2, k_k, k_a, r_k, ln_x_w, ln_x_b, w_oa, w_or, w_out, norm_ffn2, ffn2_w_in, ffn2_w_out, norm_final):
    B, S, D = x.shape
    CT = ctx.shape[1]
    assert w_mod.shape[0] == 1, "single layer"
    ATT_Q, ATT_KV = N_Q_HEADS * HEAD_DIM, N_KV_HEADS * HEAD_DIM
    assert w_in.shape[2] == ATT_Q + 2 * ATT_KV + RWKV_IN_W + 2 * D
    assert S % CHUNK == 0 and CT % CHUNK == 0 and S % GRID_W == 0

    lat_row = (S, lambda i, tm: (i * tm) // S)
    ctx_row = (B * CT, lambda i, tm: B)
    bf = lambda w: w.astype(BF16)

    modtab = _mod_table(c, c_ctx, w_mod[0], b_mod[0])

    x2 = x.reshape(B * S, D)
    cx2 = ctx.reshape(B * CT, D)
    w1_in, w1_out = bf(ffn1_w_in[0]), bf(ffn1_w_out[0])
    x2, h2 = _ffn(x2, modtab, lat_row, 0, norm_ffn1[0], w1_in, w1_out, next_norm=(norm_mix[0], 3))
    _, hc2 = _ffn(cx2, modtab, ctx_row, 0, norm_ffn1[0], w1_in, w1_out, next_norm=(norm_mix[0], 3))

    wi = w_in[0]
    o = 0
    wq = wi[:, o:o + ATT_Q]; o += ATT_Q
    wk = wi[:, o:o + ATT_KV]; o += ATT_KV
    wv = wi[:, o:o + ATT_KV]; o += ATT_KV
    wr = wi[:, o:o + RWKV_IN_W]; o += RWKV_IN_W
    wg = wi[:, o:]
    q_seg = ("headnorm_rope", ATT_Q, BF16, q_norm[0], HEAD_DIM ** -0.5 * LOG2E)
    v_seg = ("head_ones", ATT_KV, BF16, None, 1.0)
    r_seg = ("plain", RWKV_IN_W, F32, None, 1.0)
    g_seg = ("sigmoid", 2 * D, BF16, None, 1.0)
    q, k, v, p_rw, gates = _proj(h2, _proj_weight([wq, wk, wv, wr, wg]), S,
                                 [q_seg, ("headnorm_rope", ATT_KV, BF16, k_norm[0], 1.0), v_seg, r_seg, g_seg],
                                 rope_tables=_rope_tables(S))
    kc, vc, p_rw_c = _proj(hc2, _proj_weight([wk, wv, wr]), B * CT,
                           [("headnorm", ATT_KV, BF16, k_norm[0], 1.0), v_seg, r_seg])

    attn_o = _attention(q.reshape(B, S, ATT_Q), k.reshape(B, S, ATT_KV), v.reshape(B, S, 2 * ATT_KV),
                        kc.reshape(B, CT, ATT_KV), vc.reshape(B, CT, 2 * ATT_KV)).reshape(B * S, ATT_Q)

    rw = (rwkv_mu[0], w0[0], w2[0], a0[0], a2[0], g2[0], k_k[0], k_a[0], r_k[0])
    rc, vcr, akc, lwc, kmc, bbc, _, _ = _rwkv_prep(p_rw_c.reshape(B, CT, -1), *rw)
    s0 = jnp.zeros((2, B, RWKV_W // LANES, LANES, LANES), F32)
    _, _, s_ctx = _rwkv_scan(rc, vcr, akc, lwc, kmc, bbc, s0)
    rl, vl, akl, lwl, kml, bbl, gl, bvl = _rwkv_prep(p_rw.reshape(B, S, -1), *rw)
    yf, yb, _ = _rwkv_scan(rl, vl, akl, lwl, kml, bbl, s_ctx)
    rwkv_o = _rwkv_out(yf, yb, bvl, gl, ln_x_w[0], ln_x_b[0]).reshape(B * S, RWKV_W)

    t = _merge(attn_o, rwkv_o, bf(w_oa[0]), bf(w_or[0]), gates)
    x2 = _outproj(t, bf(w_out[0]), x2, modtab, lat_row, 5)

    out = _ffn(x2, modtab, lat_row, 6, norm_ffn2[0], bf(ffn2_w_in[0]), bf(ffn2_w_out[0]), g_final=norm_final)
    return out.reshape(B, S, D)
```
